```python
import jax, jax.numpy as jnp
from jax import lax
import numpy as np


D_MODEL = 2048
BATCH = 2
SEQ = 8192
DEPTH = 1

CHUNK = 64
HG_HEADS = 8
HG_DK = 128
HG_DV = 128
HG_WIDTH = HG_HEADS * HG_DK
SA_HEADS = 8
SA_HEAD_DIM = 128
SA_WIDTH = SA_HEADS * SA_HEAD_DIM
IDX_HEADS = 8
IDX_DIM = 64
MAX_TOPK = 256
N_BRANCH = 2
N_GROUPS = 4
EXPERTS_PER_GROUP = 8
N_EXPERTS = N_GROUPS * EXPERTS_PER_GROUP
TOPK_IN_GROUP = 2
D_FF_EXPERT = 1024
MOE_BLOCK = 128
DN_ALPHA = (2.0 * DEPTH) ** 0.25
DN_BETA = (8.0 * DEPTH) ** -0.25
LN_EPS = 1e-5
RMS_EPS = 1e-6
PROJ_SIZES = (HG_WIDTH, HG_WIDTH, HG_WIDTH, HG_WIDTH,
              SA_WIDTH, SA_WIDTH, SA_WIDTH,
              IDX_HEADS * IDX_DIM, IDX_DIM, IDX_HEADS,
              N_BRANCH * D_MODEL)
PROJ_TOTAL = sum(PROJ_SIZES)

kernel_name = 'hybrid_hgrn2_dsa_hmoe_block'


def _layer_norm(x, g, b):
    xf = x.astype(jnp.float32)
    mu = jnp.mean(xf, -1, keepdims=True)
    var = jnp.mean(jnp.square(xf - mu), -1, keepdims=True)
    y = (xf - mu) * lax.rsqrt(var + LN_EPS) * g.astype(jnp.float32) + b.astype(jnp.float32)
    return y.astype(x.dtype)


def _hgrn2(q, f_logit, i, g, lb, norm_g):
    bsz, seq, _ = q.shape
    n_chunks = seq // CHUNK
    f32 = jnp.float32
    shp_k = (bsz, n_chunks, CHUNK, HG_HEADS, HG_DK)
    shp_v = (bsz, n_chunks, CHUNK, HG_HEADS, HG_DV)
    f = lb + (1.0 - lb) * jax.nn.sigmoid(f_logit.astype(f32))
    qh = jax.nn.silu(q.astype(f32)).reshape(shp_k)
    kh = (1.0 - f).reshape(shp_k)
    vh = i.astype(f32).reshape(shp_v)
    cum = jnp.cumsum(jnp.log(f).reshape(shp_k), axis=2)
    cum_last = cum[:, :, -1:]
    q_dec = qh * jnp.exp(cum)
    k_dec = kh * jnp.exp(-cum)
    k_end = kh * jnp.exp(cum_last - cum)
    causal = jnp.tril(jnp.ones((CHUNK, CHUNK), dtype=bool))
    att = jnp.einsum('bnthd,bnshd->bnhts', q_dec, k_dec)
    att = jnp.where(causal, att, 0.0)
    o = jnp.einsum('bnhts,bnshv->bnthv', att, vh)
    chunk_kv = jnp.einsum('bnshd,bnshv->bnhdv', k_end, vh)
    chunk_decay = jnp.exp(cum_last[:, :, 0])

    def step(state, inp):
        dec, kv = inp
        return dec[..., None] * state + kv, state

    s0 = jnp.zeros((bsz, HG_HEADS, HG_DK, HG_DV), f32)
    _, s_prev = lax.scan(step, s0, (jnp.moveaxis(chunk_decay, 1, 0), jnp.moveaxis(chunk_kv, 1, 0)))
    s_prev = jnp.moveaxis(s_prev, 0, 1)
    o = o + jnp.einsum('bnthd,bnhdv->bnthv', q_dec, s_prev)
    o = o * lax.rsqrt(jnp.mean(jnp.square(o), -1, keepdims=True) + RMS_EPS)
    o = o.reshape(bsz, seq, HG_HEADS * HG_DV) * norm_g.astype(f32) * jax.nn.silu(g.astype(f32))
    return o.astype(q.dtype)


def _dsa(q, k, v, q_idx, k_idx, w_idx):
    bsz, seq, _ = q.shape
    n_chunks = seq // CHUNK
    topk = min(MAX_TOPK, seq // 4)
    f32 = jnp.float32
    q = q.reshape(bsz, seq, SA_HEADS, SA_HEAD_DIM)
    k = k.reshape(bsz, seq, SA_HEADS, SA_HEAD_DIM)
    v = v.reshape(bsz, seq, SA_HEADS, SA_HEAD_DIM)
    q_idx = q_idx.reshape(bsz, seq, IDX_HEADS, IDX_DIM)
    w_idx = w_idx * (IDX_HEADS ** -0.5 * IDX_DIM ** -0.5)
    scale = SA_HEAD_DIM ** -0.5
    slopes = 2.0 ** (-8.0 * jnp.arange(1, SA_HEADS + 1, dtype=f32) / SA_HEADS)
    key_chunk = jnp.arange(seq) // CHUNK
    gather = jax.vmap(lambda arr, idx: arr[idx])

    def one_chunk(c):
        start = c * CHUNK
        qc = lax.dynamic_slice_in_dim(q, start, CHUNK, axis=1)
        qic = lax.dynamic_slice_in_dim(q_idx, start, CHUNK, axis=1)
        wc = lax.dynamic_slice_in_dim(w_idx, start, CHUNK, axis=1)
        rel = jax.nn.relu(jnp.einsum('bqhd,bsd->bqhs', qic, k_idx))
        score = jnp.einsum('bqh,bqhs->bqs', wc, rel).astype(f32)
        score = jnp.where(key_chunk <= c, score, -jnp.inf)
        top_val, top_idx = lax.top_k(score, topk)
        valid = top_val > -jnp.inf
        ks = gather(k, top_idx)
        vs = gather(v, top_idx)
        qpos = start + jnp.arange(CHUNK)
        dist = jnp.abs(qpos[None, :, None] - top_idx).astype(f32)
        logits = jnp.einsum('bqhd,bqkhd->bqhk', qc, ks).astype(f32) * scale
        logits = logits - slopes[None, None, :, None] * dist[:, :, None, :]
        logits = jnp.where(valid[:, :, None, :], logits, -jnp.inf)
        p = jax.nn.softmax(logits, axis=-1)
        return jnp.einsum('bqhk,bqkhd->bqhd', p.astype(v.dtype), vs)

    out = lax.map(one_chunk, jnp.arange(n_chunks))
    return jnp.transpose(out, (1, 0, 2, 3, 4)).reshape(bsz, seq, SA_WIDTH)


def _token_mixer(x, w_in, b_in, lb, hg_norm_g, w_branch_a, w_branch_b, w_out):
    proj = x @ w_in + b_in
    points = [int(p) for p in np.cumsum(PROJ_SIZES)[:-1]]
    aq, af, ai, ag, bq, bk, bv, iq, ik, iw, gates = jnp.split(proj, points, axis=-1)
    branch_a = _hgrn2(aq, af, ai, ag, lb, hg_norm_g)
    branch_b = _dsa(bq, bk, bv, iq, ik, iw)
    gate = jax.nn.sigmoid(gates.astype(jnp.float32)).astype(x.dtype)
    gate_a, gate_b = jnp.split(gate, N_BRANCH, axis=-1)
    merged = gate_a * (branch_a @ w_branch_a) + gate_b * (branch_b @ w_branch_b)
    return merged @ w_out


def _routed_experts(xf, e_ids, wts, w_gate, w_up, w_down):
    n_tok, d = xf.shape
    n_assign = n_tok * TOPK_IN_GROUP
    e = e_ids.reshape(-1)
    tok = jnp.repeat(jnp.arange(n_tok, dtype=jnp.int32), TOPK_IN_GROUP)
    w = wts.reshape(-1)
    order = jnp.argsort(e)
    e_s, tok_s, w_s = e[order], tok[order], w[order]
    counts = jnp.zeros((N_EXPERTS,), jnp.int32).at[e].add(1)
    padded = ((counts + MOE_BLOCK - 1) // MOE_BLOCK) * MOE_BLOCK
    start = jnp.cumsum(counts) - counts
    pend = jnp.cumsum(padded)
    pstart = pend - padded
    dest = pstart[e_s] + (jnp.arange(n_assign, dtype=jnp.int32) - start[e_s])
    n_blocks = -(-(n_assign + N_EXPERTS * (MOE_BLOCK - 1)) // MOE_BLOCK)
    n_rows = n_blocks * MOE_BLOCK
    row_tok = jnp.zeros((n_rows,), jnp.int32).at[dest].set(tok_s)
    row_w = jnp.zeros((n_rows,), xf.dtype).at[dest].set(w_s)
    block_pos = jnp.arange(n_blocks, dtype=jnp.int32) * MOE_BLOCK
    block_e = jnp.minimum(jnp.searchsorted(pend, block_pos, side='right'), N_EXPERTS - 1)

    def one_block(args):
        rows, ex = args
        xb = xf[rows]
        hb = jax.nn.silu(xb @ w_gate[ex]) * (xb @ w_up[ex])
        return hb @ w_down[ex]

    ys = lax.map(one_block, (row_tok.reshape(n_blocks, MOE_BLOCK), block_e))
    ys = ys.reshape(n_rows, d) * row_w[:, None]
    return jnp.zeros_like(xf).at[row_tok].add(ys)


def _hier_moe(x, w_group, b_group, w_router, b_router, w_gate, w_up, w_down):
    bsz, seq, d = x.shape
    xf = x.reshape(-1, d)
    n_tok = xf.shape[0]
    g_prob = jax.nn.softmax((xf @ w_group + b_group).astype(jnp.float32), axis=-1)
    g_top_p, g_top = lax.top_k(g_prob, 1)
    e_logits = (xf @ w_router + b_router).astype(jnp.float32).reshape(n_tok, N_GROUPS, EXPERTS_PER_GROUP)
    e_in_group = jnp.take_along_axis(e_logits, g_top[:, :, None], axis=1)[:, 0]
    e_val, e_idx = lax.top_k(e_in_group, TOPK_IN_GROUP)
    wts = jax.nn.softmax(e_val, axis=-1) * g_top_p
    e_ids = g_top * EXPERTS_PER_GROUP + e_idx
    y = _routed_experts(xf, e_ids, wts.astype(x.dtype), w_gate, w_up, w_down)
    return y.reshape(bsz, seq, d)


def setup_inputs(seed: int = 0) -> dict:
    key = jax.random.key(seed)
    ks = jax.random.split(key, 20)
    f32 = jnp.float32
    d = D_MODEL
    nrm = lambda k, shp, s: jax.random.normal(k, shp, f32) * s
    x = nrm(ks[0], (BATCH, SEQ, d), 1.0)
    w_in = nrm(ks[1], (DEPTH, d, PROJ_TOTAL), d ** -0.5)
    v_lo = 4 * HG_WIDTH + 2 * SA_WIDTH
    w_in = w_in.at[:, :, v_lo:v_lo + SA_WIDTH].multiply(DN_BETA)
    b_in = nrm(ks[2], (DEPTH, PROJ_TOTAL), 0.02)
    hg_lb_logits = nrm(ks[3], (DEPTH + 1, HG_WIDTH), 0.1)
    hg_norm_g = 1.0 + nrm(ks[4], (DEPTH, HG_WIDTH), 0.02)
    w_branch_a = nrm(ks[5], (DEPTH, HG_WIDTH, d), HG_WIDTH ** -0.5)
    w_branch_b = nrm(ks[6], (DEPTH, SA_WIDTH, d), SA_WIDTH ** -0.5)
    w_out = nrm(ks[7], (DEPTH, d, d), d ** -0.5 * DN_BETA)
    ln1_g = 1.0 + nrm(ks[8], (DEPTH, d), 0.02)
    ln1_b = nrm(ks[9], (DEPTH, d), 0.02)
    w_group = nrm(ks[10], (DEPTH, d, N_GROUPS), d ** -0.5)
    b_group = nrm(ks[11], (DEPTH, N_GROUPS), 0.01)
    w_router = nrm(ks[12], (DEPTH, d, N_EXPERTS), d ** -0.5)
    b_router = nrm(ks[13], (DEPTH, N_EXPERTS), 0.01)
    w_gate = nrm(ks[14], (DEPTH, N_EXPERTS, d, D_FF_EXPERT), d ** -0.5)
    w_up = nrm(ks[15], (DEPTH, N_EXPERTS, d, D_FF_EXPERT), d ** -0.5)
    w_down = nrm(ks[16], (DEPTH, N_EXPERTS, D_FF_EXPERT, d), D_FF_EXPERT ** -0.5 * DN_BETA)
    ln2_g = 1.0 + nrm(ks[17], (DEPTH, d), 0.02)
    ln2_b = nrm(ks[18], (DEPTH, d), 0.02)
    return {'x': x, 'w_in': w_in, 'b_in': b_in, 'hg_lb_logits': hg_lb_logits, 'hg_norm_g': hg_norm_g,
            'w_branch_a': w_branch_a, 'w_branch_b': w_branch_b, 'w_out': w_out,
            'ln1_g': ln1_g, 'ln1_b': ln1_b, 'w_group': w_group, 'b_group': b_group,
            'w_router': w_router, 'b_router': b_router, 'w_gate': w_gate, 'w_up': w_up,
            'w_down': w_down, 'ln2_g': ln2_g, 'ln2_b': ln2_b}


def reference(x, w_in, b_in, hg_lb_logits, hg_norm_g, w_branch_a, w_branch_b, w_out,
              ln1_g, ln1_b, w_group, b_group, w_router, b_router, w_gate, w_up,
              w_down, ln2_g, ln2_b):
    lower_bounds = jnp.cumsum(jax.nn.softmax(hg_lb_logits.astype(jnp.float32), axis=0), axis=0)
    h = x
    for l in range(DEPTH):
        mix = _token_mixer(h, w_in[l], b_in[l], lower_bounds[l], hg_norm_g[l],
                           w_branch_a[l], w_branch_b[l], w_out[l])
        h = _layer_norm(DN_ALPHA * h + mix, ln1_g[l], ln1_b[l])
        ffn = _hier_moe(h, w_group[l], b_group[l], w_router[l], b_router[l],
                        w_gate[l], w_up[l], w_down[l])
        h = _layer_norm(DN_ALPHA * h + ffn, ln2_g[l], ln2_b[l])
    return h
```

```python
import functools

import jax
import jax.numpy as jnp
from jax import lax
from jax.experimental import pallas as pl
from jax.experimental.pallas import tpu as pltpu

CHUNK = 64
HG_HEADS = 8
HG_DK = 128
HG_DV = 128
HG_WIDTH = HG_HEADS * HG_DK
SA_HEADS = 8
SA_HEAD_DIM = 128
SA_WIDTH = SA_HEADS * SA_HEAD_DIM
IDX_HEADS = 8
IDX_DIM = 64
MAX_TOPK = 256
N_GROUPS = 4
EXPERTS_PER_GROUP = 8
N_EXPERTS = N_GROUPS * EXPERTS_PER_GROUP
TOPK_IN_GROUP = 2
DEPTH = 1
DN_ALPHA = (2.0 * DEPTH) ** 0.25
LN_EPS = 1e-5
RMS_EPS = 1e-6

LANES = 128
VMEM_LIMIT = 56 * 1024 * 1024
NEG_BIG = -1e30

F32 = jnp.float32
BF16 = jnp.bfloat16


def _round_up(a, m):
    return (a + m - 1) // m * m


def _nt_dot(a, b):
    return lax.dot_general(a, b, (((1,), (1,)), ((), ())), preferred_element_type=F32)


def _tn_dot(a, b):
    return lax.dot_general(a, b, (((0,), (0,)), ((), ())), preferred_element_type=F32)


def _sigmoid(x):
    return 1.0 / (1.0 + jnp.exp(-x))


def _silu(x):
    return x * _sigmoid(x)


def _proj_layout(d):
    gate_w = _round_up(2 * d, HG_WIDTH)
    off = {"gates": 0}
    o = gate_w
    for name in ("aq", "af", "ai", "ag", "bq", "bk", "bv"):
        off[name] = o
        o += HG_WIDTH
    off["iq"] = o
    o += IDX_HEADS * IDX_DIM
    off["ikw"] = o
    o += LANES
    return off, o


def _proj_kernel(x_ref, w_ref, b_ref, o_ref, xb_ref):
    @pl.when(pl.program_id(1) == 0)
    def _():
        xb_ref[...] = x_ref[...].astype(BF16)

    o_ref[...] = jnp.dot(xb_ref[...], w_ref[...], preferred_element_type=F32) + b_ref[...]


def _in_proj(x2, w_r, b_r, tm, tn):
    n, d = x2.shape
    np_ = w_r.shape[1]
    return pl.pallas_call(
        _proj_kernel,
        grid=(n // tm, np_ // tn),
        in_specs=[
            pl.BlockSpec((tm, d), lambda i, j: (i, 0)),
            pl.BlockSpec((d, tn), lambda i, j: (0, j)),
            pl.BlockSpec((1, tn), lambda i, j: (0, j)),
        ],
        out_specs=pl.BlockSpec((tm, tn), lambda i, j: (i, j)),
        out_shape=jax.ShapeDtypeStruct((n, np_), F32),
        scratch_shapes=[pltpu.VMEM((tm, d), BF16)],
        name="in_proj",
        compiler_params=pltpu.CompilerParams(
            dimension_semantics=("parallel", "arbitrary"), vmem_limit_bytes=VMEM_LIMIT),
    )(x2, w_r, b_r)


def _hgrn_kernel(q_ref, f_ref, i_ref, g_ref, lbl_ref, ng_ref, o_ref, st_ref, *, n_chunks):
    @pl.when(pl.program_id(1) == 0)
    def _():
        st_ref[...] = jnp.zeros_like(st_ref)

    lbl = lbl_ref[...].astype(F32)
    lbe = jnp.exp(lbl - jnp.max(lbl, axis=0, keepdims=True))
    lb = lbe[0:1, :] / jnp.sum(lbe, axis=0, keepdims=True)

    r_i = lax.broadcasted_iota(jnp.int32, (CHUNK, CHUNK), 0)
    c_i = lax.broadcasted_iota(jnp.int32, (CHUNK, CHUNK), 1)
    causal = c_i <= r_i
    tri = jnp.where(causal, 1.0, 0.0).astype(BF16)

    def chunk_body(c, carry):
        r0 = pl.multiple_of(c * CHUNK, CHUNK)
        rows = pl.ds(r0, CHUNK)
        for h in range(HG_HEADS):
            cols = slice(h * HG_DK, (h + 1) * HG_DK)
            lbh = lb[:, cols]
            f = lbh + (1.0 - lbh) * _sigmoid(f_ref[rows, cols])
            logf = jnp.log(f)
            hi = logf.astype(BF16)
            lo = (logf - hi.astype(F32)).astype(BF16)
            cum = (jnp.dot(tri, hi, preferred_element_type=F32)
                   + jnp.dot(tri, lo, preferred_element_type=F32))
            cum_last = cum[CHUNK - 1:CHUNK, :]
            qh = _silu(q_ref[rows, cols])
            kh = 1.0 - f
            vh = i_ref[rows, cols].astype(BF16)
            q_dec = (qh * jnp.exp(cum)).astype(BF16)
            k_dec = (kh * jnp.exp(-cum)).astype(BF16)
            k_end = (kh * jnp.exp(cum_last - cum)).astype(BF16)
            att = jnp.where(causal, _nt_dot(q_dec, k_dec), 0.0)
            st = st_ref[h]
            o = (jnp.dot(att.astype(BF16), vh, preferred_element_type=F32)
                 + _nt_dot(q_dec, st.astype(BF16)))
            st_ref[h] = st * jnp.exp(cum_last) + _tn_dot(vh, k_end)
            o = o * lax.rsqrt(jnp.mean(o * o, axis=-1, keepdims=True) + RMS_EPS)
            o_ref[rows, cols] = o * ng_ref[:, cols] * _silu(g_ref[rows, cols])
        return carry

    lax.fori_loop(0, n_chunks, chunk_body, 0)


def _hgrn(proj, off, lb_logits, norm_g, bsz, seq, rows_per_step):
    n = bsz * seq
    w = HG_WIDTH
    steps = seq // rows_per_step
    nl = lb_logits.shape[0]

    def col_spec(name):
        cb = off[name] // w
        return pl.BlockSpec((rows_per_step, w), lambda b, s, cb=cb: (b * steps + s, cb))

    return pl.pallas_call(
        functools.partial(_hgrn_kernel, n_chunks=rows_per_step // CHUNK),
        grid=(bsz, steps),
        in_specs=[col_spec("aq"), col_spec("af"), col_spec("ai"), col_spec("ag"),
                  pl.BlockSpec((nl, w), lambda b, s: (0, 0)),
                  pl.BlockSpec((1, w), lambda b, s: (0, 0))],
        out_specs=pl.BlockSpec((rows_per_step, w), lambda b, s: (b * steps + s, 0)),
        out_shape=jax.ShapeDtypeStruct((n, w), F32),
        scratch_shapes=[pltpu.VMEM((HG_HEADS, HG_DV, HG_DK), F32)],
        name="hgrn",
        compiler_params=pltpu.CompilerParams(
            dimension_semantics=("parallel", "arbitrary"), vmem_limit_bytes=VMEM_LIMIT),
    )(proj, proj, proj, proj, lb_logits, norm_g)


def _dsa_kernel(qi_ref, qw_ref, q_ref, kiw_ref, k_ref, v_ref, o_ref,
                s_scr, m_scr, l_scr, acc_scr, *, tq, tk, rg, topk, t0, t_more):
    i = pl.program_id(1)
    nkb = (i + 1) * (tq // tk)
    q0 = i * tq
    nl = tk // LANES

    w = qw_ref[:, IDX_DIM:IDX_DIM + IDX_HEADS].astype(F32) * (IDX_HEADS ** -0.5 * IDX_DIM ** -0.5)

    row_t = lax.broadcasted_iota(jnp.int32, (tq, tk), 0)
    col_t = lax.broadcasted_iota(jnp.int32, (tq, tk), 1)
    qchunk_t = (q0 + row_t) // CHUNK

    def score_body(kb, carry):
        k0 = pl.multiple_of(kb * tk, tk)
        kblk = kiw_ref[pl.ds(k0, tk), :][:, :IDX_DIM]
        acc = jnp.zeros((tq, tk), F32)
        for h in range(IDX_HEADS):
            d = _nt_dot(qi_ref[:, h * IDX_DIM:(h + 1) * IDX_DIM], kblk)
            acc = acc + w[:, h:h + 1] * jnp.maximum(d, 0.0)
        adm = (k0 + col_t) // CHUNK <= qchunk_t
        s_scr[:, pl.ds(k0, tk)] = jnp.where(adm, acc, -jnp.inf)
        return carry

    lax.fori_loop(0, nkb, score_body, 0)

    for g in range(tq // rg):
        rows = pl.ds(g * rg, rg)
        row_g = lax.broadcasted_iota(jnp.int32, (rg, LANES), 0)
        col_g = lax.broadcasted_iota(jnp.int32, (rg, LANES), 1)
        n_adm = ((q0 + g * rg + row_g[:, :1]) // CHUNK + 1) * CHUNK
        kk = jnp.minimum(n_adm, topk).astype(F32)

        def reduce_blocks(fn, init):
            def body(kb, c):
                k0 = pl.multiple_of(kb * tk, tk)
                for j in range(nl):
                    c = fn(c, s_scr[rows, pl.ds(k0 + j * LANES, LANES)], k0 + j * LANES)
                return c
            return lax.fori_loop(0, nkb, body, init)

        def count(pred_fn):
            c = reduce_blocks(lambda c, blk, k0: c + jnp.where(pred_fn(blk, k0), 1.0, 0.0),
                              jnp.zeros((rg, LANES), F32))
            return jnp.sum(c, axis=-1, keepdims=True)

        def bcast(x):
            return jnp.broadcast_to(x, (rg, LANES))

        mx = jnp.max(reduce_blocks(lambda c, blk, k0: jnp.maximum(c, blk),
                                   jnp.full((rg, LANES), -jnp.inf, F32)), axis=-1, keepdims=True)
        mn = jnp.min(reduce_blocks(lambda c, blk, k0: jnp.minimum(c, jnp.where(blk == -jnp.inf, jnp.inf, blk)),
                                   jnp.full((rg, LANES), jnp.inf, F32)), axis=-1, keepdims=True)
        mxb = bcast(mx)
        c_top = count(lambda blk, k0: blk >= mxb)
        lo0 = jnp.where(c_top >= kk, mx, mn)
        hi0 = mx

        def bisect(n_it, lo, hi):
            def it(_, lh):
                lo, hi = lh
                mid = 0.5 * lo + 0.5 * hi
                midb = bcast(mid)
                c = count(lambda blk, k0: blk >= midb)
                ok = c >= kk
                return jnp.where(ok, mid, lo), jnp.where(ok, hi, mid)
            return lax.fori_loop(0, n_it, it, (lo, hi))

        def verify(lo):
            lob = bcast(lo)
            thr = jnp.min(reduce_blocks(
                lambda c, blk, k0: jnp.minimum(c, jnp.where(blk >= lob, blk, jnp.inf)),
                jnp.full((rg, LANES), jnp.inf, F32)), axis=-1, keepdims=True)
            thrb = bcast(thr)
            n_gt = count(lambda blk, k0: blk > thrb)
            return thr, n_gt

        lo, hi = bisect(t0, lo0, hi0)
        thr, n_gt = verify(lo)

        def w_cond(st):
            lo, hi, thr, n_gt, it = st
            return jnp.logical_and(jnp.max(jnp.where(n_gt >= kk, 1.0, 0.0)) > 0.0, it < 64)

        def w_body(st):
            lo, hi, thr, n_gt, it = st
            lo, hi = bisect(t_more, lo, hi)
            thr, n_gt = verify(lo)
            return lo, hi, thr, n_gt, it + 1

        lo, hi, thr, n_gt, _ = lax.while_loop(w_cond, w_body, (lo, hi, thr, n_gt, jnp.int32(0)))

        thrb = bcast(thr)
        need = kk - n_gt
        n_ge = count(lambda blk, k0: blk >= thrb)
        excess = jnp.max(jnp.where(n_ge > kk, 1.0, 0.0)) > 0.0
        cut_all = jnp.full((rg, 1), 2 * s_scr.shape[1], jnp.int32)

        def tie_cut():
            nbits = (2 * s_scr.shape[1] - 1).bit_length()

            def it(b, c):
                cand = c + jnp.left_shift(jnp.int32(1), nbits - 1 - b)
                candb = bcast(cand)
                g_c = count(lambda blk, k0: jnp.logical_and(blk == thrb, (k0 + col_g) < candb))
                return jnp.where(g_c <= need, cand, c)
            return lax.fori_loop(0, nbits, it, jnp.zeros((rg, 1), jnp.int32))

        cut = lax.cond(excess, tie_cut, lambda: cut_all)
        cutb = bcast(cut)

        def mask_body(kb, carry):
            k0 = pl.multiple_of(kb * tk, tk)
            for j in range(nl):
                sl = pl.ds(k0 + j * LANES, LANES)
                blk = s_scr[rows, sl]
                sel = jnp.logical_or(blk > thrb,
                                     jnp.logical_and(blk == thrb, (k0 + j * LANES + col_g) < cutb))
                s_scr[rows, sl] = jnp.where(sel, 0.0, NEG_BIG)
            return carry

        lax.fori_loop(0, nkb, mask_body, 0)

    m_scr[...] = jnp.full(m_scr.shape, NEG_BIG, F32)
    l_scr[...] = jnp.zeros(l_scr.shape, F32)
    acc_scr[...] = jnp.zeros(acc_scr.shape, F32)
    scale = SA_HEAD_DIM ** -0.5

    def att_body(kb, carry):
        k0 = pl.multiple_of(kb * tk, tk)
        bias = s_scr[:, pl.ds(k0, tk)]
        dist = jnp.abs(q0 + row_t - (k0 + col_t)).astype(F32)
        for h in range(SA_HEADS):
            cols = slice(h * SA_HEAD_DIM, (h + 1) * SA_HEAD_DIM)
            slope = 2.0 ** (-8.0 * (h + 1) / SA_HEADS)
            logits = _nt_dot(q_ref[:, cols], k_ref[pl.ds(k0, tk), cols]) * scale - slope * dist + bias
            m_old = m_scr[h]
            m_new = jnp.maximum(m_old, jnp.max(logits, axis=-1, keepdims=True))
            alpha = jnp.exp(m_old - m_new)
            p = jnp.exp(logits - m_new)
            l_scr[h] = alpha * l_scr[h] + jnp.sum(p, axis=-1, keepdims=True)
            acc_scr[h] = alpha * acc_scr[h] + jnp.dot(p.astype(BF16), v_ref[pl.ds(k0, tk), cols],
                                                      preferred_element_type=F32)
            m_scr[h] = m_new
        return carry

    lax.fori_loop(0, nkb, att_body, 0)

    for h in range(SA_HEADS):
        o_ref[:, h * SA_HEAD_DIM:(h + 1) * SA_HEAD_DIM] = acc_scr[h] / l_scr[h]


def _dsa(qi, qkw, q, k, v, bsz, seq, tq, tk, rg):
    n = bsz * seq
    steps = seq // tq
    topk = min(MAX_TOPK, seq // 4)
    once = dict(pipeline_mode=pl.Buffered(1))
    return pl.pallas_call(
        functools.partial(_dsa_kernel, tq=tq, tk=tk, rg=rg, topk=topk, t0=26, t_more=6),
        grid=(bsz, steps),
        in_specs=[
            pl.BlockSpec((tq, IDX_HEADS * IDX_DIM), lambda b, i: (b * steps + i, 0)),
            pl.BlockSpec((tq, LANES), lambda b, i: (b * steps + i, 0)),
            pl.BlockSpec((tq, SA_WIDTH), lambda b, i: (b * steps + i, 0)),
            pl.BlockSpec((seq, LANES), lambda b, i: (b, 0), **once),
            pl.BlockSpec((seq, SA_WIDTH), lambda b, i: (b, 0), **once),
            pl.BlockSpec((seq, SA_WIDTH), lambda b, i: (b, 0), **once),
        ],
        out_specs=pl.BlockSpec((tq, SA_WIDTH), lambda b, i: (b * steps + i, 0)),
        out_shape=jax.ShapeDtypeStruct((n, SA_WIDTH), F32),
        scratch_shapes=[
            pltpu.VMEM((tq, seq), F32),
            pltpu.VMEM((SA_HEADS, tq, 1), F32),
            pltpu.VMEM((SA_HEADS, tq, 1), F32),
            pltpu.VMEM((SA_HEADS, tq, SA_HEAD_DIM), F32),
        ],
        name="dsa",
        compiler_params=pltpu.CompilerParams(
            dimension_semantics=("parallel", "arbitrary"), vmem_limit_bytes=VMEM_LIMIT),
    )(qi, qkw, q, qkw, k, v)


def _layer_norm(x, g, b):
    mu = jnp.mean(x, axis=-1, keepdims=True)
    xc = x - mu
    var = jnp.mean(xc * xc, axis=-1, keepdims=True)
    return xc * lax.rsqrt(var + LN_EPS) * g + b


def _merge_kernel(a_ref, b_ref, ga_ref, gb_ref, x_ref, wa_ref, wb_ref, wo_ref, g1_ref, b1_ref,
                  wrh_ref, wrl_ref, br_ref, h_ref, ids_ref, wts_ref):
    merged = (_sigmoid(ga_ref[...]) * jnp.dot(a_ref[...].astype(BF16), wa_ref[...], preferred_element_type=F32)
              + _sigmoid(gb_ref[...]) * jnp.dot(b_ref[...].astype(BF16), wb_ref[...], preferred_element_type=F32))
    mix = jnp.dot(merged.astype(BF16), wo_ref[...], preferred_element_type=F32)
    h = _layer_norm(DN_ALPHA * x_ref[...] + mix, g1_ref[...], b1_ref[...])
    h_ref[...] = h

    h_hi = h.astype(BF16)
    h_lo = (h - h_hi.astype(F32)).astype(BF16)
    logits = (jnp.dot(h_hi, wrh_ref[...], preferred_element_type=F32)
              + jnp.dot(h_lo, wrh_ref[...], preferred_element_type=F32)
              + jnp.dot(h_hi, wrl_ref[...], preferred_element_type=F32)) + br_ref[...]
    lane = lax.broadcasted_iota(jnp.int32, logits.shape, 1)
    big = jnp.int32(4 * LANES)

    def first_argmax(vals):
        mval = jnp.max(vals, axis=-1, keepdims=True)
        idx = jnp.min(jnp.where(vals == mval, lane, big), axis=-1, keepdims=True)
        return mval, idx

    gl = jnp.where(lane < N_GROUPS, logits, -jnp.inf)
    g_max, g_idx = first_argmax(gl)
    g_top_p = 1.0 / jnp.sum(jnp.exp(gl - g_max), axis=-1, keepdims=True)
    e_lo = N_GROUPS + g_idx * EXPERTS_PER_GROUP
    el = jnp.where(jnp.logical_and(lane >= e_lo, lane < e_lo + EXPERTS_PER_GROUP), logits, -jnp.inf)
    e1, i1 = first_argmax(el)
    e2, i2 = first_argmax(jnp.where(lane == i1, -jnp.inf, el))
    t = jnp.exp(e2 - e1)
    w1 = (1.0 / (1.0 + t)) * g_top_p
    w2 = (t / (1.0 + t)) * g_top_p
    ids_ref[...] = jnp.where(lane == 0, i1 - N_GROUPS, jnp.where(lane == 1, i2 - N_GROUPS, 0))
    wts_ref[...] = jnp.where(lane == 0, w1, jnp.where(lane == 1, w2, 0.0))


def _merge(branch_a, branch_b, proj, off, x2, wa, wb, wo, g1, b1, wr_hi, wr_lo, br, tm):
    n, d = x2.shape
    once = dict(pipeline_mode=pl.Buffered(1))
    ga_blk = off["gates"] // d
    row = lambda i: (i, 0)
    const = lambda i: (0, 0)
    return pl.pallas_call(
        _merge_kernel,
        grid=(n // tm,),
        in_specs=[
            pl.BlockSpec((tm, HG_WIDTH), row),
            pl.BlockSpec((tm, SA_WIDTH), row),
            pl.BlockSpec((tm, d), lambda i: (i, ga_blk)),
            pl.BlockSpec((tm, d), lambda i: (i, ga_blk + 1)),
            pl.BlockSpec((tm, d), row),
            pl.BlockSpec((HG_WIDTH, d), const, **once),
            pl.BlockSpec((SA_WIDTH, d), const, **once),
            pl.BlockSpec((d, d), const, **once),
            pl.BlockSpec((1, d), const),
            pl.BlockSpec((1, d), const),
            pl.BlockSpec((d, LANES), const, **once),
            pl.BlockSpec((d, LANES), const, **once),
            pl.BlockSpec((1, LANES), const),
        ],
        out_specs=[pl.BlockSpec((tm, d), row), pl.BlockSpec((tm, LANES), row), pl.BlockSpec((tm, LANES), row)],
        out_shape=[jax.ShapeDtypeStruct((n, d), F32),
                   jax.ShapeDtypeStruct((n, LANES), jnp.int32),
                   jax.ShapeDtypeStruct((n, LANES), F32)],
        name="merge_ln1_router",
        compiler_params=pltpu.CompilerParams(
            dimension_semantics=("parallel",), vmem_limit_bytes=VMEM_LIMIT),
    )(branch_a, branch_b, proj, proj, x2, wa, wb, wo, g1, b1, wr_hi, wr_lo, br)


def _ffn_kernel(be_ref, nu_ref, x_ref, wg_ref, wu_ref, wd_ref, o_ref):
    @pl.when(pl.program_id(0) < nu_ref[0])
    def _():
        xb = x_ref[...].astype(BF16)
        g = jnp.dot(xb, wg_ref[0], preferred_element_type=F32)
        u = jnp.dot(xb, wu_ref[0], preferred_element_type=F32)
        hb = (_silu(g) * u).astype(BF16)
        o_ref[...] = jnp.dot(hb, wd_ref[0], preferred_element_type=F32)

    @pl.when(pl.program_id(0) >= nu_ref[0])
    def _():
        o_ref[...] = jnp.zeros_like(o_ref)


def _moe_ffn(xs, block_e, n_used, wg, wu, wd, bm):
    n_rows, d = xs.shape
    f = wg.shape[2]
    n_blocks = n_rows // bm
    grid_spec = pltpu.PrefetchScalarGridSpec(
        num_scalar_prefetch=2,
        grid=(n_blocks,),
        in_specs=[
            pl.BlockSpec((bm, d), lambda i, be, nu: (i, 0)),
            pl.BlockSpec((1, d, f), lambda i, be, nu: (be[i], 0, 0)),
            pl.BlockSpec((1, d, f), lambda i, be, nu: (be[i], 0, 0)),
            pl.BlockSpec((1, f, d), lambda i, be, nu: (be[i], 0, 0)),
        ],
        out_specs=pl.BlockSpec((bm, d), lambda i, be, nu: (i, 0)),
    )
    return pl.pallas_call(
        _ffn_kernel,
        grid_spec=grid_spec,
        out_shape=jax.ShapeDtypeStruct((n_rows, d), F32),
        name="moe_ffn",
        compiler_params=pltpu.CompilerParams(
            dimension_semantics=("arbitrary",), vmem_limit_bytes=VMEM_LIMIT),
    )(block_e, n_used, xs, wg, wu, wd)


def _combine_kernel(h_ref, y0_ref, y1_ref, wts_ref, g2_ref, b2_ref, o_ref):
    w0 = wts_ref[:, 0:1]
    w1 = wts_ref[:, 1:2]
    ffn = w0 * y0_ref[...] + w1 * y1_ref[...]
    o_ref[...] = _layer_norm(DN_ALPHA * h_ref[...] + ffn, g2_ref[...], b2_ref[...])


def _combine(h1, y0, y1, wts, g2, b2, tm):
    n, d = h1.shape
    row = lambda i: (i, 0)
    const = lambda i: (0, 0)
    return pl.pallas_call(
        _combine_kernel,
        grid=(n // tm,),
        in_specs=[pl.BlockSpec((tm, d), row), pl.BlockSpec((tm, d), row), pl.BlockSpec((tm, d), row),
                  pl.BlockSpec((tm, LANES), row), pl.BlockSpec((1, d), const), pl.BlockSpec((1, d), const)],
        out_specs=pl.BlockSpec((tm, d), row),
        out_shape=jax.ShapeDtypeStruct((n, d), F32),
        name="combine_ln2",
        compiler_params=pltpu.CompilerParams(
            dimension_semantics=("parallel",), vmem_limit_bytes=VMEM_LIMIT),
    )(h1, y0, y1, wts, g2, b2)


def _dispatch_plan(e_ids, bm):
    n_tok = e_ids.shape[0]
    n_assign = n_tok * TOPK_IN_GROUP
    e = e_ids.reshape(-1)
    onehot = (e[:, None] == jnp.arange(N_EXPERTS, dtype=jnp.int32)[None, :]).astype(jnp.int32)
    cum = jnp.cumsum(onehot, axis=0)
    rank = jnp.take_along_axis(cum, e[:, None], axis=1)[:, 0] - 1
    counts = cum[-1]
    padded = (counts + bm - 1) // bm * bm
    pend = jnp.cumsum(padded)
    pstart = pend - padded
    dest = pstart[e] + rank
    n_blocks = -(-(n_assign + N_EXPERTS * (bm - 1)) // bm)
    n_rows = n_blocks * bm
    tok = jnp.repeat(jnp.arange(n_tok, dtype=jnp.int32), TOPK_IN_GROUP)
    row_tok = jnp.zeros((n_rows,), jnp.int32).at[dest].set(tok)
    block_pos = jnp.arange(n_blocks, dtype=jnp.int32) * bm
    block_e = jnp.sum((pend[None, :] <= block_pos[:, None]).astype(jnp.int32), axis=1)
    block_e = jnp.minimum(block_e, N_EXPERTS - 1)
    n_used = (pend[-1] // bm).astype(jnp.int32).reshape(1)
    return row_tok, dest.reshape(n_tok, TOPK_IN_GROUP), block_e, n_used


def kernel(x, w_in, b_in, hg_lb_logits, hg_norm_g, w_branch_a, w_branch_b, w_out, ln1_g, ln1_b,
           w_group, b_group, w_router, b_router, w_gate, w_up, w_down, ln2_g, ln2_b):
    bsz, seq, d = x.shape
    n = bsz * seq
    x2 = x.reshape(n, d)
    l = 0

    off, total = _proj_layout(d)
    tn = 1024
    np_ = _round_up(total, tn)
    sizes = (HG_WIDTH,) * 4 + (SA_WIDTH,) * 3 + (IDX_HEADS * IDX_DIM, IDX_DIM, IDX_HEADS, 2 * d)
    starts = [0]
    for s_ in sizes:
        starts.append(starts[-1] + s_)
    names = ("aq", "af", "ai", "ag", "bq", "bk", "bv", "iq", "ik", "iw", "gates")
    col = {nm: (starts[j], starts[j + 1]) for j, nm in enumerate(names)}

    def place(src):
        out = jnp.zeros(src.shape[:-1] + (np_,), src.dtype)
        for nm in ("gates", "aq", "af", "ai", "ag", "bq", "bk", "bv", "iq"):
            a, b = col[nm]
            out = lax.dynamic_update_slice_in_dim(out, src[..., a:b], off[nm], axis=-1)
        a, b = col["ik"][0], col["iw"][1]
        return lax.dynamic_update_slice_in_dim(out, src[..., a:b], off["ikw"], axis=-1)

    w_r = place(w_in[l]).astype(BF16)
    b_r = place(b_in[l]).reshape(1, np_)

    tm_proj = min(1024, n)
    proj = _in_proj(x2, w_r, b_r, tm_proj, tn)

    rows_per_step = min(512, seq)
    branch_a = _hgrn(proj, off, hg_lb_logits, hg_norm_g[l].reshape(1, HG_WIDTH), bsz, seq, rows_per_step)

    def sect(name, width):
        return proj[:, off[name]:off[name] + width].astype(BF16)

    tq = min(256, seq)
    branch_b = _dsa(sect("iq", IDX_HEADS * IDX_DIM), sect("ikw", LANES), sect("bq", SA_WIDTH),
                    sect("bk", SA_WIDTH), sect("bv", SA_WIDTH), bsz, seq, tq=tq, tk=min(256, tq), rg=min(128, tq))

    wr = jnp.zeros((d, LANES), F32)
    wr = wr.at[:, :N_GROUPS].set(w_group[l]).at[:, N_GROUPS:N_GROUPS + N_EXPERTS].set(w_router[l])
    wr_hi = wr.astype(BF16)
    wr_lo = (wr - wr_hi.astype(F32)).astype(BF16)
    br = jnp.zeros((1, LANES), F32)
    br = br.at[0, :N_GROUPS].set(b_group[l]).at[0, N_GROUPS:N_GROUPS + N_EXPERTS].set(b_router[l])
    tm = min(256, n)
    h1, ids, wts = _merge(branch_a, branch_b, proj, off, x2,
                          w_branch_a[l].astype(BF16), w_branch_b[l].astype(BF16), w_out[l].astype(BF16),
                          ln1_g[l].reshape(1, d), ln1_b[l].reshape(1, d), wr_hi, wr_lo, br, tm)

    bm = 256
    row_tok, dest, block_e, n_used = _dispatch_plan(ids[:, :TOPK_IN_GROUP], bm)
    xs = jnp.take(h1, row_tok, axis=0)
    ys = _moe_ffn(xs, block_e, n_used, w_gate[l].astype(BF16), w_up[l].astype(BF16), w_down[l].astype(BF16), bm)
    y0 = jnp.take(ys, dest[:, 0], axis=0)
    y1 = jnp.take(ys, dest[:, 1], axis=0)
    out = _combine(h1, y0, y1, wts, ln2_g[l].reshape(1, d), ln2_b[l].reshape(1, d), tm)
    return out.reshape(bsz, seq, d)
```

```python
import functools
import math

import jax
import jax.numpy as jnp
from jax import lax
from jax.experimental import pallas as pl
from jax.experimental.pallas import tpu as pltpu

CHUNK = 64
HG_HEADS = 8
HG_DK = 128
HG_DV = 128
HG_WIDTH = HG_HEADS * HG_DK
SA_HEADS = 8
SA_HEAD_DIM = 128
SA_WIDTH = SA_HEADS * SA_HEAD_DIM
IDX_HEADS = 8
IDX_DIM = 64
IDX_WIDTH = IDX_HEADS * IDX_DIM
MAX_TOPK = 256
N_GROUPS = 4
EXPERTS_PER_GROUP = 8
N_EXPERTS = N_GROUPS * EXPERTS_PER_GROUP
TOPK_IN_GROUP = 2
DEPTH = 1
DN_ALPHA = (2.0 * DEPTH) ** 0.25
LN_EPS = 1e-5
RMS_EPS = 1e-6

LANES = 128
SUBLANES = 8
VMEM_LIMIT = 56 * 1024 * 1024
MASK_DIST = 1e30
M_INIT = -1e30

F32 = jnp.float32
BF16 = jnp.bfloat16

PROJ_TN = 512
MAIN_COLS = 4 * HG_WIDTH + 3 * SA_WIDTH + IDX_WIDTH
N_MAIN = MAIN_COLS // PROJ_TN
J16_LO = 4 * HG_WIDTH // PROJ_TN
J16_HI = J16_LO + (3 * SA_WIDTH + IDX_WIDTH) // PROJ_TN + 1
O16_COLS = (J16_HI - J16_LO) * PROJ_TN
O16_Q, O16_K, O16_V, O16_IQ, O16_IKW = 0, SA_WIDTH, 2 * SA_WIDTH, 3 * SA_WIDTH, 3 * SA_WIDTH + IDX_WIDTH
O32_GATES = 4 * HG_WIDTH


def _nt_dot(a, b):
    return lax.dot_general(a, b, (((1,), (1,)), ((), ())), preferred_element_type=F32)


def _tn_dot(a, b):
    return lax.dot_general(a, b, (((0,), (0,)), ((), ())), preferred_element_type=F32)


def _sigmoid(x):
    return 1.0 / (1.0 + jnp.exp(-x))


def _silu(x):
    return x * _sigmoid(x)


def _proj_kernel(x_ref, wm_ref, wt_ref, bm_ref, bt_ref, o32_ref, o16_ref, xb_ref):
    j = pl.program_id(1)

    @pl.when(j == 0)
    def _():
        xb_ref[...] = x_ref[...].astype(BF16)

    def block(w_ref, b_ref):
        return jnp.dot(xb_ref[...], w_ref[...], preferred_element_type=F32) + b_ref[...]

    in16 = jnp.logical_and(j >= J16_LO, j < J16_HI)

    @pl.when(j < J16_LO)
    def _():
        o32_ref[...] = block(wm_ref, bm_ref)

    @pl.when(jnp.logical_and(in16, j < N_MAIN))
    def _():
        o16_ref[...] = block(wm_ref, bm_ref).astype(BF16)

    @pl.when(jnp.logical_and(in16, j >= N_MAIN))
    def _():
        o16_ref[...] = block(wt_ref, bt_ref).astype(BF16)

    @pl.when(j >= J16_HI)
    def _():
        o32_ref[...] = block(wt_ref, bt_ref)


def _in_proj(x2, w_main, b_main, w_tail, b_tail, tm):
    n, d = x2.shape
    tn = PROJ_TN
    n_tail = w_tail.shape[1] // tn
    nj = N_MAIN + n_tail
    o32_cols = (nj - (J16_HI - J16_LO)) * tn

    def o32_map(i, j):
        return i, jnp.where(j < J16_LO, j, jnp.maximum(j, J16_HI) - (J16_HI - J16_LO))

    return pl.pallas_call(
        _proj_kernel,
        grid=(n // tm, nj),
        in_specs=[
            pl.BlockSpec((tm, d), lambda i, j: (i, 0)),
            pl.BlockSpec((d, tn), lambda i, j: (0, jnp.minimum(j, N_MAIN - 1))),
            pl.BlockSpec((d, tn), lambda i, j: (0, jnp.maximum(j - N_MAIN, 0))),
            pl.BlockSpec((1, tn), lambda i, j: (0, jnp.minimum(j, N_MAIN - 1))),
            pl.BlockSpec((1, tn), lambda i, j: (0, jnp.maximum(j - N_MAIN, 0))),
        ],
        out_specs=[
            pl.BlockSpec((tm, tn), o32_map),
            pl.BlockSpec((tm, tn), lambda i, j: (i, jnp.clip(j - J16_LO, 0, J16_HI - J16_LO - 1))),
        ],
        out_shape=[jax.ShapeDtypeStruct((n, o32_cols), F32), jax.ShapeDtypeStruct((n, O16_COLS), BF16)],
        scratch_shapes=[pltpu.VMEM((tm, d), BF16)],
        name="in_proj",
        compiler_params=pltpu.CompilerParams(
            dimension_semantics=("parallel", "arbitrary"), vmem_limit_bytes=VMEM_LIMIT),
    )(x2, w_main, w_tail, b_main, b_tail)


def _hgrn_kernel(q_ref, f_ref, i_ref, g_ref, lbl_ref, ng_ref, o_ref, st_ref, *, n_chunks):
    @pl.when(pl.program_id(1) == 0)
    def _():
        st_ref[...] = jnp.zeros_like(st_ref)

    lbl = lbl_ref[...].astype(F32)
    lbe = jnp.exp(lbl - jnp.max(lbl, axis=0, keepdims=True))
    lb = lbe[0:1, :] / jnp.sum(lbe, axis=0, keepdims=True)

    r_i = lax.broadcasted_iota(jnp.int32, (CHUNK, CHUNK), 0)
    c_i = lax.broadcasted_iota(jnp.int32, (CHUNK, CHUNK), 1)
    causal = c_i <= r_i
    tri = jnp.where(causal, 1.0, 0.0).astype(BF16)

    def chunk_body(c, carry):
        r0 = pl.multiple_of(c * CHUNK, CHUNK)
        rows = pl.ds(r0, CHUNK)
        for h in range(HG_HEADS):
            cols = slice(h * HG_DK, (h + 1) * HG_DK)
            lbh = lb[:, cols]
            f = lbh + (1.0 - lbh) * _sigmoid(f_ref[rows, cols])
            logf = jnp.log(f)
            hi = logf.astype(BF16)
            lo = (logf - hi.astype(F32)).astype(BF16)
            cum = (jnp.dot(tri, hi, preferred_element_type=F32)
                   + jnp.dot(tri, lo, preferred_element_type=F32))
            cum_last = cum[CHUNK - 1:CHUNK, :]
            qh = _silu(q_ref[rows, cols])
            kh = 1.0 - f
            vh = i_ref[rows, cols].astype(BF16)
            q_dec = (qh * jnp.exp(cum)).astype(BF16)
            k_dec = (kh * jnp.exp(-cum)).astype(BF16)
            k_end = (kh * jnp.exp(cum_last - cum)).astype(BF16)
            att = jnp.where(causal, _nt_dot(q_dec, k_dec), 0.0)
            st = st_ref[h]
            o = (jnp.dot(att.astype(BF16), vh, preferred_element_type=F32)
                 + _nt_dot(q_dec, st.astype(BF16)))
            st_ref[h] = st * jnp.exp(cum_last) + _tn_dot(vh, k_end)
            o = o * lax.rsqrt(jnp.mean(o * o, axis=-1, keepdims=True) + RMS_EPS)
            o_ref[rows, cols] = o * ng_ref[:, cols] * _silu(g_ref[rows, cols])
        return carry

    lax.fori_loop(0, n_chunks, chunk_body, 0)


def _hgrn(p32, lb_logits, norm_g, bsz, seq, rows_per_step):
    n = bsz * seq
    w = HG_WIDTH
    steps = seq // rows_per_step
    nl = lb_logits.shape[0]

    def col_spec(cb):
        return pl.BlockSpec((rows_per_step, w), lambda b, s: (b * steps + s, cb))

    return pl.pallas_call(
        functools.partial(_hgrn_kernel, n_chunks=rows_per_step // CHUNK),
        grid=(bsz, steps),
        in_specs=[col_spec(0), col_spec(1), col_spec(2), col_spec(3),
                  pl.BlockSpec((nl, w), lambda b, s: (0, 0)),
                  pl.BlockSpec((1, w), lambda b, s: (0, 0))],
        out_specs=pl.BlockSpec((rows_per_step, w), lambda b, s: (b * steps + s, 0)),
        out_shape=jax.ShapeDtypeStruct((n, w), F32),
        scratch_shapes=[pltpu.VMEM((HG_HEADS, HG_DV, HG_DK), F32)],
        name="hgrn",
        compiler_params=pltpu.CompilerParams(
            dimension_semantics=("parallel", "arbitrary"), vmem_limit_bytes=VMEM_LIMIT),
    )(p32, p32, p32, p32, lb_logits, norm_g)


def _dsa_kernel(qi_ref, qw_ref, q_ref, kiw_ref, k_ref, vt_ref, o_ref,
                s_scr, qs_scr, *acc_scrs, tq, tk, topk, t0, t_more, hg):
    i = pl.program_id(1)
    nkb = (i + 1) * (tq // tk)
    q0 = i * tq
    seq = s_scr.shape[0]
    nsub = tk // SUBLANES

    key_t = lax.broadcasted_iota(jnp.int32, (tk, tq), 0)
    qry_t = q0 + lax.broadcasted_iota(jnp.int32, (tk, tq), 1)
    qchunk_t = qry_t // CHUNK
    qcol = q0 + lax.broadcasted_iota(jnp.int32, (1, tq), 1)
    kk = jnp.minimum((qcol // CHUNK + 1) * CHUNK, topk).astype(F32)

    w_t = (jnp.transpose(qw_ref[...].astype(F32))[IDX_DIM:IDX_DIM + IDX_HEADS, :]
           * (IDX_HEADS ** -0.5 * IDX_DIM ** -0.5))

    def tree(op, x):
        while x.shape[0] > 1:
            half = x.shape[0] // 2
            x = op(x[:half], x[half:])
        return x[0]

    def by_vreg(x):
        return x.reshape(nsub, SUBLANES, tq)

    def score_body(kb, carry):
        mn8, mx8 = carry
        k0 = pl.multiple_of(kb * tk, tk)
        kblk = kiw_ref[pl.ds(k0, tk), :][:, :IDX_DIM]
        ds = [_nt_dot(kblk, qi_ref[:, h * IDX_DIM:(h + 1) * IDX_DIM]) for h in range(IDX_HEADS)]
        acc = w_t[0:1, :] * jnp.maximum(ds[0], 0.0)
        for h in range(1, IDX_HEADS):
            acc = acc + w_t[h:h + 1, :] * jnp.maximum(ds[h], 0.0)
        adm = (k0 + key_t) // CHUNK <= qchunk_t
        s_scr[pl.ds(k0, tk), :] = jnp.where(adm, acc, -jnp.inf)
        mn8 = jnp.minimum(mn8, tree(jnp.minimum, by_vreg(jnp.where(adm, acc, jnp.inf))))
        mx8 = jnp.maximum(mx8, tree(jnp.maximum, by_vreg(jnp.where(adm, acc, -jnp.inf))))
        return mn8, mx8

    mn8, mx8 = lax.fori_loop(0, nkb, score_body, (jnp.full((SUBLANES, tq), jnp.inf, F32),
                                                  jnp.full((SUBLANES, tq), -jnp.inf, F32)))
    mn = jnp.min(mn8, axis=0, keepdims=True)
    mx = jnp.max(mx8, axis=0, keepdims=True)

    def fold(fn, init):
        def body(kb, c):
            k0 = pl.multiple_of(kb * tk, tk)
            return fn(c, s_scr[pl.ds(k0, tk), :], k0)
        return lax.fori_loop(0, nkb, body, init)

    def count(pred_fn):
        c = fold(lambda c, blk, k0: c + tree(jnp.add, by_vreg(jnp.where(pred_fn(blk, k0), 1.0, 0.0))),
                 jnp.zeros((SUBLANES, tq), F32))
        return jnp.sum(c, axis=0, keepdims=True)

    def any_lane(pred):
        return jnp.max(jnp.where(pred, 1.0, 0.0)) > 0.0

    def bisect(n_it, lo, hi, c_lo):
        def it(_, st):
            lo, hi, c_lo = st
            mid = 0.5 * lo + 0.5 * hi
            c = count(lambda blk, k0: blk >= mid)
            ok = c >= kk
            return jnp.where(ok, mid, lo), jnp.where(ok, hi, mid), jnp.where(ok, c, c_lo)
        return lax.fori_loop(0, n_it, it, (lo, hi, c_lo))

    def b_cond(st):
        lo, hi, c_lo, it = st
        return jnp.logical_and(any_lane(c_lo != kk), it < t0)

    def b_body(st):
        lo, hi, c_lo, it = st
        lo, hi, c_lo = bisect(2, lo, hi, c_lo)
        return lo, hi, c_lo, it + 2

    n_adm = ((qcol // CHUNK + 1) * CHUNK).astype(F32)
    lo, hi, c_lo, _ = lax.while_loop(b_cond, b_body, (mn, mx, n_adm, jnp.int32(0)))
    big_cut = jnp.full((1, tq), 2 * seq, jnp.int32)

    def exact_ties():
        def fold_min(sel_fn):
            c = fold(lambda c, blk, k0: jnp.minimum(c, tree(jnp.minimum, by_vreg(sel_fn(blk)))),
                     jnp.full((SUBLANES, tq), jnp.inf, F32))
            return jnp.min(c, axis=0, keepdims=True)

        def verify(lo):
            thr = fold_min(lambda blk: jnp.where(blk >= lo, blk, jnp.inf))
            return thr, count(lambda blk, k0: blk > thr)

        def w_cond(st):
            lo, hi, c_lo, thr, n_gt, it = st
            return jnp.logical_and(any_lane(n_gt >= kk), it < 64)

        def w_body(st):
            lo, hi, c_lo, thr, n_gt, it = st
            lo, hi, c_lo = bisect(t_more, lo, hi, c_lo)
            thr, n_gt = verify(lo)
            return lo, hi, c_lo, thr, n_gt, it + 1

        thr, n_gt = verify(lo)
        _, _, _, thr, n_gt, _ = lax.while_loop(w_cond, w_body, (lo, hi, c_lo, thr, n_gt, jnp.int32(0)))
        need = kk - n_gt
        nbits = (2 * seq - 1).bit_length()

        def it(b, c):
            cand = c + jnp.left_shift(jnp.int32(1), nbits - 1 - b)
            g_c = count(lambda blk, k0: jnp.logical_and(blk == thr, (k0 + key_t) < cand))
            return jnp.where(g_c <= need, cand, c)
        return thr, lax.fori_loop(0, nbits, it, jnp.zeros((1, tq), jnp.int32))

    thr, cut = lax.cond(any_lane(c_lo != kk), exact_ties, lambda: (lo, big_cut))

    def mask_body(kb, carry):
        k0 = pl.multiple_of(kb * tk, tk)
        blk = s_scr[pl.ds(k0, tk), :]
        sel = jnp.logical_or(blk > thr, jnp.logical_and(blk == thr, (k0 + key_t) < cut))
        s_scr[pl.ds(k0, tk), :] = jnp.where(sel, 0.0, MASK_DIST)
        return carry

    lax.fori_loop(0, nkb, mask_body, 0)

    log2e = math.log2(math.e)
    qs_scr[...] = (q_ref[...].astype(F32) * (SA_HEAD_DIM ** -0.5 * log2e)).astype(BF16)
    for acc_scr in acc_scrs:
        acc_scr[...] = jnp.zeros(acc_scr.shape, F32)

    def att_body(kb, carry):
        m_all, l_all = carry
        k0 = pl.multiple_of(kb * tk, tk)
        dist = jnp.abs(qry_t - (k0 + key_t)).astype(F32) + s_scr[pl.ds(k0, tk), :]
        cols = [slice(h * SA_HEAD_DIM, (h + 1) * SA_HEAD_DIM) for h in range(SA_HEADS)]
        m_rows, l_rows = [], []
        for g0 in range(0, SA_HEADS, hg):
            heads = range(g0, g0 + hg)
            qk = {h: _nt_dot(k_ref[pl.ds(k0, tk), cols[h]], qs_scr[:, cols[h]]) for h in heads}
            alphas, ps = {}, {}
            for h in heads:
                slope2 = 2.0 ** (-8.0 * (h + 1) / SA_HEADS) * log2e
                logits = qk[h] - slope2 * dist
                m_old = m_all[h:h + 1, :]
                m_new = jnp.maximum(m_old, jnp.max(logits, axis=0, keepdims=True))
                alphas[h] = jnp.exp2(m_old - m_new)
                p = jnp.exp2(logits - m_new)
                l_rows.append(alphas[h] * l_all[h:h + 1, :] + jnp.sum(p, axis=0, keepdims=True))
                m_rows.append(m_new)
                ps[h] = p.astype(BF16)
            pv = {h: jnp.dot(vt_ref[cols[h], pl.ds(k0, tk)], ps[h], preferred_element_type=F32) for h in heads}
            for h in heads:
                acc_scrs[h][...] = alphas[h] * acc_scrs[h][...] + pv[h]
        return jnp.concatenate(m_rows, axis=0), jnp.concatenate(l_rows, axis=0)

    _, l_all = lax.fori_loop(0, nkb, att_body,
                             (jnp.full((SA_HEADS, tq), M_INIT, F32), jnp.zeros((SA_HEADS, tq), F32)))

    for h in range(SA_HEADS):
        o_ref[:, h * SA_HEAD_DIM:(h + 1) * SA_HEAD_DIM] = jnp.transpose(acc_scrs[h][...] / l_all[h:h + 1, :])


def _dsa(p16, v_t, bsz, seq, tq, tk):
    n = bsz * seq
    steps = seq // tq
    topk = min(MAX_TOPK, seq // 4)
    once = dict(pipeline_mode=pl.Buffered(1))
    qrow = lambda cb: (lambda b, i: (b * steps + i, cb))
    return pl.pallas_call(
        functools.partial(_dsa_kernel, tq=tq, tk=tk, topk=topk, t0=28, t_more=4, hg=SA_HEADS),
        grid=(bsz, steps),
        in_specs=[
            pl.BlockSpec((tq, IDX_WIDTH), qrow(O16_IQ // IDX_WIDTH)),
            pl.BlockSpec((tq, LANES), qrow(O16_IKW // LANES)),
            pl.BlockSpec((tq, SA_WIDTH), qrow(O16_Q // SA_WIDTH)),
            pl.BlockSpec((seq, LANES), lambda b, i: (b, O16_IKW // LANES), **once),
            pl.BlockSpec((seq, SA_WIDTH), lambda b, i: (b, O16_K // SA_WIDTH), **once),
            pl.BlockSpec((SA_WIDTH, seq), lambda b, i: (b, 0), **once),
        ],
        out_specs=pl.BlockSpec((tq, SA_WIDTH), lambda b, i: (b * steps + i, 0)),
        out_shape=jax.ShapeDtypeStruct((n, SA_WIDTH), F32),
        scratch_shapes=[
            pltpu.VMEM((seq, tq), F32),
            pltpu.VMEM((tq, SA_WIDTH), BF16),
        ] + [pltpu.VMEM((SA_HEAD_DIM, tq), F32) for _ in range(SA_HEADS)],
        name="dsa",
        compiler_params=pltpu.CompilerParams(
            dimension_semantics=("parallel", "arbitrary"), vmem_limit_bytes=VMEM_LIMIT),
    )(p16, p16, p16, p16, p16, v_t)


def _layer_norm(x, g, b):
    mu = jnp.mean(x, axis=-1, keepdims=True)
    xc = x - mu
    var = jnp.mean(xc * xc, axis=-1, keepdims=True)
    return xc * lax.rsqrt(var + LN_EPS) * g + b


def _merge_kernel(a_ref, b_ref, ga_ref, gb_ref, x_ref, wa_ref, wb_ref, wo_ref, g1_ref, b1_ref,
                  wrh_ref, wrl_ref, br_ref, h_ref, ids_ref, wts_ref):
    merged = (_sigmoid(ga_ref[...]) * jnp.dot(a_ref[...].astype(BF16), wa_ref[...], preferred_element_type=F32)
              + _sigmoid(gb_ref[...]) * jnp.dot(b_ref[...].astype(BF16), wb_ref[...], preferred_element_type=F32))
    mix = jnp.dot(merged.astype(BF16), wo_ref[...], preferred_element_type=F32)
    h = _layer_norm(DN_ALPHA * x_ref[...] + mix, g1_ref[...], b1_ref[...])
    h_ref[...] = h

    h_hi = h.astype(BF16)
    h_lo = (h - h_hi.astype(F32)).astype(BF16)
    logits = (jnp.dot(h_hi, wrh_ref[...], preferred_element_type=F32)
              + jnp.dot(h_lo, wrh_ref[...], preferred_element_type=F32)
              + jnp.dot(h_hi, wrl_ref[...], preferred_element_type=F32)) + br_ref[...]
    lane = lax.broadcasted_iota(jnp.int32, logits.shape, 1)
    big = jnp.int32(4 * LANES)

    def first_argmax(vals):
        mval = jnp.max(vals, axis=-1, keepdims=True)
        idx = jnp.min(jnp.where(vals == mval, lane, big), axis=-1, keepdims=True)
        return mval, idx

    gl = jnp.where(lane < N_GROUPS, logits, -jnp.inf)
    g_max, g_idx = first_argmax(gl)
    g_top_p = 1.0 / jnp.sum(jnp.exp(gl - g_max), axis=-1, keepdims=True)
    e_lo = N_GROUPS + g_idx * EXPERTS_PER_GROUP
    el = jnp.where(jnp.logical_and(lane >= e_lo, lane < e_lo + EXPERTS_PER_GROUP), logits, -jnp.inf)
    e1, i1 = first_argmax(el)
    e2, i2 = first_argmax(jnp.where(lane == i1, -jnp.inf, el))
    t = jnp.exp(e2 - e1)
    w1 = (1.0 / (1.0 + t)) * g_top_p
    w2 = (t / (1.0 + t)) * g_top_p
    ids_ref[...] = jnp.where(lane == 0, i1 - N_GROUPS, jnp.where(lane == 1, i2 - N_GROUPS, 0))
    wts_ref[...] = jnp.where(lane == 0, w1, jnp.where(lane == 1, w2, 0.0))


def _merge(branch_a, branch_b, p32, x2, wa, wb, wo, g1, b1, wr_hi, wr_lo, br, tm):
    n, d = x2.shape
    once = dict(pipeline_mode=pl.Buffered(1))
    ga_blk = O32_GATES // d
    row = lambda i: (i, 0)
    const = lambda i: (0, 0)
    return pl.pallas_call(
        _merge_kernel,
        grid=(n // tm,),
        in_specs=[
            pl.BlockSpec((tm, HG_WIDTH), row),
            pl.BlockSpec((tm, SA_WIDTH), row),
            pl.BlockSpec((tm, d), lambda i: (i, ga_blk)),
            pl.BlockSpec((tm, d), lambda i: (i, ga_blk + 1)),
            pl.BlockSpec((tm, d), row),
            pl.BlockSpec((HG_WIDTH, d), const, **once),
            pl.BlockSpec((SA_WIDTH, d), const, **once),
            pl.BlockSpec((d, d), const, **once),
            pl.BlockSpec((1, d), const),
            pl.BlockSpec((1, d), const),
            pl.BlockSpec((d, LANES), const, **once),
            pl.BlockSpec((d, LANES), const, **once),
            pl.BlockSpec((1, LANES), const),
        ],
        out_specs=[pl.BlockSpec((tm, d), row), pl.BlockSpec((tm, LANES), row), pl.BlockSpec((tm, LANES), row)],
        out_shape=[jax.ShapeDtypeStruct((n, d), F32),
                   jax.ShapeDtypeStruct((n, LANES), jnp.int32),
                   jax.ShapeDtypeStruct((n, LANES), F32)],
        name="merge_ln1_router",
        compiler_params=pltpu.CompilerParams(
            dimension_semantics=("parallel",), vmem_limit_bytes=VMEM_LIMIT),
    )(branch_a, branch_b, p32, p32, x2, wa, wb, wo, g1, b1, wr_hi, wr_lo, br)


def _moe_up_kernel(be_ref, nu_ref, x_ref, wg_ref, wu_ref, o_ref):
    @pl.when(pl.program_id(0) < nu_ref[0])
    def _():
        xb = x_ref[...].astype(BF16)
        g = jnp.dot(xb, wg_ref[0], preferred_element_type=F32)
        u = jnp.dot(xb, wu_ref[0], preferred_element_type=F32)
        o_ref[...] = (_silu(g) * u).astype(BF16)

    @pl.when(pl.program_id(0) >= nu_ref[0])
    def _():
        o_ref[...] = jnp.zeros_like(o_ref)


def _moe_down_kernel(be_ref, nu_ref, h_ref, wd_ref, o_ref):
    @pl.when(pl.program_id(0) < nu_ref[0])
    def _():
        o_ref[...] = jnp.dot(h_ref[...], wd_ref[0], preferred_element_type=F32)

    @pl.when(pl.program_id(0) >= nu_ref[0])
    def _():
        o_ref[...] = jnp.zeros_like(o_ref)


def _moe_ffn(xs, block_e, n_used, wg, wu, wd, bm):
    n_rows, d = xs.shape
    f = wg.shape[2]
    n_blocks = n_rows // bm
    rows = lambda i, be, nu: (i, 0)
    expert = lambda i, be, nu: (be[i], 0, 0)
    params = pltpu.CompilerParams(dimension_semantics=("arbitrary",), vmem_limit_bytes=VMEM_LIMIT)
    hb = pl.pallas_call(
        _moe_up_kernel,
        grid_spec=pltpu.PrefetchScalarGridSpec(
            num_scalar_prefetch=2, grid=(n_blocks,),
            in_specs=[pl.BlockSpec((bm, d), rows), pl.BlockSpec((1, d, f), expert), pl.BlockSpec((1, d, f), expert)],
            out_specs=pl.BlockSpec((bm, f), rows)),
        out_shape=jax.ShapeDtypeStruct((n_rows, f), BF16),
        name="moe_up", compiler_params=params,
    )(block_e, n_used, xs, wg, wu)
    return pl.pallas_call(
        _moe_down_kernel,
        grid_spec=pltpu.PrefetchScalarGridSpec(
            num_scalar_prefetch=2, grid=(n_blocks,),
            in_specs=[pl.BlockSpec((bm, f), rows), pl.BlockSpec((1, f, d), expert)],
            out_specs=pl.BlockSpec((bm, d), rows)),
        out_shape=jax.ShapeDtypeStruct((n_rows, d), F32),
        name="moe_down", compiler_params=params,
    )(block_e, n_used, hb, wd)


def _combine_kernel(h_ref, y0_ref, y1_ref, wts_ref, g2_ref, b2_ref, o_ref):
    w0 = wts_ref[:, 0:1]
    w1 = wts_ref[:, 1:2]
    ffn = w0 * y0_ref[...] + w1 * y1_ref[...]
    o_ref[...] = _layer_norm(DN_ALPHA * h_ref[...] + ffn, g2_ref[...], b2_ref[...])


def _combine(h1, y0, y1, wts, g2, b2, tm):
    n, d = h1.shape
    row = lambda i: (i, 0)
    const = lambda i: (0, 0)
    return pl.pallas_call(
        _combine_kernel,
        grid=(n // tm,),
        in_specs=[pl.BlockSpec((tm, d), row), pl.BlockSpec((tm, d), row), pl.BlockSpec((tm, d), row),
                  pl.BlockSpec((tm, LANES), row), pl.BlockSpec((1, d), const), pl.BlockSpec((1, d), const)],
        out_specs=pl.BlockSpec((tm, d), row),
        out_shape=jax.ShapeDtypeStruct((n, d), F32),
        name="combine_ln2",
        compiler_params=pltpu.CompilerParams(
            dimension_semantics=("parallel",), vmem_limit_bytes=VMEM_LIMIT),
    )(h1, y0, y1, wts, g2, b2)


def _dispatch_plan(e_ids, bm):
    n_tok = e_ids.shape[0]
    n_assign = n_tok * TOPK_IN_GROUP
    e = e_ids.reshape(-1)
    onehot = (e[:, None] == jnp.arange(N_EXPERTS, dtype=jnp.int32)[None, :]).astype(jnp.int32)
    cum = jnp.cumsum(onehot, axis=0)
    rank = jnp.take_along_axis(cum, e[:, None], axis=1)[:, 0] - 1
    counts = cum[-1]
    padded = (counts + bm - 1) // bm * bm
    pend = jnp.cumsum(padded)
    pstart = pend - padded
    dest = pstart[e] + rank
    n_blocks = -(-(n_assign + N_EXPERTS * (bm - 1)) // bm)
    n_rows = n_blocks * bm
    tok = jnp.repeat(jnp.arange(n_tok, dtype=jnp.int32), TOPK_IN_GROUP)
    row_tok = jnp.zeros((n_rows,), jnp.int32).at[dest].set(tok)
    block_pos = jnp.arange(n_blocks, dtype=jnp.int32) * bm
    block_e = jnp.sum((pend[None, :] <= block_pos[:, None]).astype(jnp.int32), axis=1)
    block_e = jnp.minimum(block_e, N_EXPERTS - 1)
    n_used = (pend[-1] // bm).astype(jnp.int32).reshape(1)
    return row_tok, dest.reshape(n_tok, TOPK_IN_GROUP), block_e, n_used


def kernel(x, w_in, b_in, hg_lb_logits, hg_norm_g, w_branch_a, w_branch_b, w_out, ln1_g, ln1_b,
           w_group, b_group, w_router, b_router, w_gate, w_up, w_down, ln2_g, ln2_b):
    bsz, seq, d = x.shape
    n = bsz * seq
    x2 = x.reshape(n, d)
    l = 0

    def tail(src):
        kw = src[..., MAIN_COLS:MAIN_COLS + IDX_DIM + IDX_HEADS]
        pad = jnp.zeros(src.shape[:-1] + (PROJ_TN - IDX_DIM - IDX_HEADS,), src.dtype)
        return jnp.concatenate([kw, pad, src[..., MAIN_COLS + IDX_DIM + IDX_HEADS:]], axis=-1)

    p32, p16 = _in_proj(x2, w_in[l], b_in[l].reshape(1, -1), tail(w_in[l]), tail(b_in[l]).reshape(1, -1),
                        tm=min(1024, n))

    branch_a = _hgrn(p32, hg_lb_logits, hg_norm_g[l].reshape(1, HG_WIDTH), bsz, seq, min(512, seq))

    v_t = p16[:, O16_V:O16_V + SA_WIDTH].reshape(bsz, seq, SA_WIDTH).transpose(0, 2, 1).reshape(bsz * SA_WIDTH, seq)
    tq = min(256, seq)
    branch_b = _dsa(p16, v_t, bsz, seq, tq=tq, tk=min(256, tq))

    wr = jnp.zeros((d, LANES), F32)
    wr = wr.at[:, :N_GROUPS].set(w_group[l]).at[:, N_GROUPS:N_GROUPS + N_EXPERTS].set(w_router[l])
    wr_hi = wr.astype(BF16)
    wr_lo = (wr - wr_hi.astype(F32)).astype(BF16)
    br = jnp.zeros((1, LANES), F32)
    br = br.at[0, :N_GROUPS].set(b_group[l]).at[0, N_GROUPS:N_GROUPS + N_EXPERTS].set(b_router[l])
    tm = min(256, n)
    h1, ids, wts = _merge(branch_a, branch_b, p32, x2,
                          w_branch_a[l].astype(BF16), w_branch_b[l].astype(BF16), w_out[l].astype(BF16),
                          ln1_g[l].reshape(1, d), ln1_b[l].reshape(1, d), wr_hi, wr_lo, br, tm)

    bm = 256
    row_tok, dest, block_e, n_used = _dispatch_plan(ids[:, :TOPK_IN_GROUP], bm)
    xs = jnp.take(h1, row_tok, axis=0)
    ys = _moe_ffn(xs, block_e, n_used, w_gate[l], w_up[l], w_down[l], bm)
    y0 = jnp.take(ys, dest[:, 0], axis=0)
    y1 = jnp.take(ys, dest[:, 1], axis=0)
    out = _combine(h1, y0, y1, wts, ln2_g[l].reshape(1, d), ln2_b[l].reshape(1, d), tm)
    return out.reshape(bsz, seq, d)
```

```python
import functools
import math

import jax
import jax.numpy as jnp
from jax import lax
from jax.experimental import pallas as pl
from jax.experimental.pallas import tpu as pltpu

CHUNK = 64
HG_HEADS = 8
HG_DK = 128
HG_DV = 128
HG_WIDTH = HG_HEADS * HG_DK
SA_HEADS = 8
SA_HEAD_DIM = 128
SA_WIDTH = SA_HEADS * SA_HEAD_DIM
IDX_HEADS = 8
IDX_DIM = 64
IDX_WIDTH = IDX_HEADS * IDX_DIM
MAX_TOPK = 256
N_GROUPS = 4
EXPERTS_PER_GROUP = 8
N_EXPERTS = N_GROUPS * EXPERTS_PER_GROUP
TOPK_IN_GROUP = 2
DEPTH = 1
DN_ALPHA = (2.0 * DEPTH) ** 0.25
LN_EPS = 1e-5
RMS_EPS = 1e-6

LANES = 128
SUBLANES = 8
VMEM_LIMIT = 56 * 1024 * 1024
MASK_DIST = 1e30
M_INIT = -1e30

F32 = jnp.float32
BF16 = jnp.bfloat16

PROJ_TN = 512
MAIN_COLS = 4 * HG_WIDTH + 3 * SA_WIDTH + IDX_WIDTH
N_MAIN = MAIN_COLS // PROJ_TN
J16_LO = 4 * HG_WIDTH // PROJ_TN
J16_HI = J16_LO + (3 * SA_WIDTH + IDX_WIDTH) // PROJ_TN + 1
O16_COLS = (J16_HI - J16_LO) * PROJ_TN
O16_Q, O16_K, O16_V, O16_IQ, O16_IKW = 0, SA_WIDTH, 2 * SA_WIDTH, 3 * SA_WIDTH, 3 * SA_WIDTH + IDX_WIDTH
O32_GATES = 4 * HG_WIDTH


def _nt_dot(a, b):
    return lax.dot_general(a, b, (((1,), (1,)), ((), ())), preferred_element_type=F32)


def _tn_dot(a, b):
    return lax.dot_general(a, b, (((0,), (0,)), ((), ())), preferred_element_type=F32)


def _sigmoid(x):
    return 1.0 / (1.0 + jnp.exp(-x))


def _silu(x):
    return x * _sigmoid(x)


def _proj_kernel(x_ref, wm_ref, wt_ref, bm_ref, bt_ref, o32_ref, o16_ref, xb_ref):
    j = pl.program_id(1)

    @pl.when(j == 0)
    def _():
        xb_ref[...] = x_ref[...].astype(BF16)

    def block(w_ref, b_ref):
        return jnp.dot(xb_ref[...], w_ref[...], preferred_element_type=F32) + b_ref[...]

    in16 = jnp.logical_and(j >= J16_LO, j < J16_HI)

    @pl.when(j < J16_LO)
    def _():
        o32_ref[...] = block(wm_ref, bm_ref)

    @pl.when(jnp.logical_and(in16, j < N_MAIN))
    def _():
        o16_ref[...] = block(wm_ref, bm_ref).astype(BF16)

    @pl.when(jnp.logical_and(in16, j >= N_MAIN))
    def _():
        o16_ref[...] = block(wt_ref, bt_ref).astype(BF16)

    @pl.when(j >= J16_HI)
    def _():
        o32_ref[...] = block(wt_ref, bt_ref)


def _in_proj(x2, w_main, b_main, w_tail, b_tail, tm):
    n, d = x2.shape
    tn = PROJ_TN
    n_tail = w_tail.shape[1] // tn
    nj = N_MAIN + n_tail
    o32_cols = (nj - (J16_HI - J16_LO)) * tn

    def o32_map(i, j):
        return i, jnp.where(j < J16_LO, j, jnp.maximum(j, J16_HI) - (J16_HI - J16_LO))

    return pl.pallas_call(
        _proj_kernel,
        grid=(n // tm, nj),
        in_specs=[
            pl.BlockSpec((tm, d), lambda i, j: (i, 0)),
            pl.BlockSpec((d, tn), lambda i, j: (0, jnp.minimum(j, N_MAIN - 1))),
            pl.BlockSpec((d, tn), lambda i, j: (0, jnp.maximum(j - N_MAIN, 0))),
            pl.BlockSpec((1, tn), lambda i, j: (0, jnp.minimum(j, N_MAIN - 1))),
            pl.BlockSpec((1, tn), lambda i, j: (0, jnp.maximum(j - N_MAIN, 0))),
        ],
        out_specs=[
            pl.BlockSpec((tm, tn), o32_map),
            pl.BlockSpec((tm, tn), lambda i, j: (i, jnp.clip(j - J16_LO, 0, J16_HI - J16_LO - 1))),
        ],
        out_shape=[jax.ShapeDtypeStruct((n, o32_cols), F32), jax.ShapeDtypeStruct((n, O16_COLS), BF16)],
        scratch_shapes=[pltpu.VMEM((tm, d), BF16)],
        name="in_proj",
        compiler_params=pltpu.CompilerParams(
            dimension_semantics=("parallel", "arbitrary"), vmem_limit_bytes=VMEM_LIMIT),
    )(x2, w_main, w_tail, b_main, b_tail)


def _hgrn_kernel(q_ref, f_ref, i_ref, g_ref, lbl_ref, ng_ref, o_ref, st_ref, *, n_chunks):
    @pl.when(pl.program_id(1) == 0)
    def _():
        st_ref[...] = jnp.zeros_like(st_ref)

    lbl = lbl_ref[...].astype(F32)
    lbe = jnp.exp(lbl - jnp.max(lbl, axis=0, keepdims=True))
    lb = lbe[0:1, :] / jnp.sum(lbe, axis=0, keepdims=True)

    r_i = lax.broadcasted_iota(jnp.int32, (CHUNK, CHUNK), 0)
    c_i = lax.broadcasted_iota(jnp.int32, (CHUNK, CHUNK), 1)
    causal = c_i <= r_i
    tri = jnp.where(causal, 1.0, 0.0).astype(BF16)

    def chunk_body(c, carry):
        r0 = pl.multiple_of(c * CHUNK, CHUNK)
        rows = pl.ds(r0, CHUNK)
        heads = range(HG_HEADS)
        cols = [slice(h * HG_DK, (h + 1) * HG_DK) for h in heads]
        f = [lb[:, cols[h]] + (1.0 - lb[:, cols[h]]) * _sigmoid(f_ref[rows, cols[h]]) for h in heads]
        logf = [jnp.log(f[h]) for h in heads]
        hi = [logf[h].astype(BF16) for h in heads]
        lo = [(logf[h] - hi[h].astype(F32)).astype(BF16) for h in heads]
        cum = [jnp.dot(tri, hi[h], preferred_element_type=F32) + jnp.dot(tri, lo[h], preferred_element_type=F32)
               for h in heads]
        cum_last = [cum[h][CHUNK - 1:CHUNK, :] for h in heads]
        vh = [i_ref[rows, cols[h]].astype(BF16) for h in heads]
        q_dec = [(_silu(q_ref[rows, cols[h]]) * jnp.exp(cum[h])).astype(BF16) for h in heads]
        k_dec = [((1.0 - f[h]) * jnp.exp(-cum[h])).astype(BF16) for h in heads]
        k_end = [((1.0 - f[h]) * jnp.exp(cum_last[h] - cum[h])).astype(BF16) for h in heads]
        st = [st_ref[h] for h in heads]
        att = [jnp.where(causal, _nt_dot(q_dec[h], k_dec[h]), 0.0).astype(BF16) for h in heads]
        o_st = [_nt_dot(q_dec[h], st[h].astype(BF16)) for h in heads]
        kv = [_tn_dot(vh[h], k_end[h]) for h in heads]
        o = [jnp.dot(att[h], vh[h], preferred_element_type=F32) + o_st[h] for h in heads]
        for h in heads:
            st_ref[h] = st[h] * jnp.exp(cum_last[h]) + kv[h]
            on = o[h] * lax.rsqrt(jnp.mean(o[h] * o[h], axis=-1, keepdims=True) + RMS_EPS)
            o_ref[rows, cols[h]] = on * ng_ref[:, cols[h]] * _silu(g_ref[rows, cols[h]])
        return carry

    lax.fori_loop(0, n_chunks, chunk_body, 0)


def _hgrn(p32, lb_logits, norm_g, bsz, seq, rows_per_step):
    n = bsz * seq
    w = HG_WIDTH
    steps = seq // rows_per_step
    nl = lb_logits.shape[0]

    def col_spec(cb):
        return pl.BlockSpec((rows_per_step, w), lambda b, s: (b * steps + s, cb))

    return pl.pallas_call(
        functools.partial(_hgrn_kernel, n_chunks=rows_per_step // CHUNK),
        grid=(bsz, steps),
        in_specs=[col_spec(0), col_spec(1), col_spec(2), col_spec(3),
                  pl.BlockSpec((nl, w), lambda b, s: (0, 0)),
                  pl.BlockSpec((1, w), lambda b, s: (0, 0))],
        out_specs=pl.BlockSpec((rows_per_step, w), lambda b, s: (b * steps + s, 0)),
        out_shape=jax.ShapeDtypeStruct((n, w), F32),
        scratch_shapes=[pltpu.VMEM((HG_HEADS, HG_DV, HG_DK), F32)],
        name="hgrn",
        compiler_params=pltpu.CompilerParams(
            dimension_semantics=("parallel", "arbitrary"), vmem_limit_bytes=VMEM_LIMIT),
    )(p32, p32, p32, p32, lb_logits, norm_g)


def _dsa_kernel(qi_ref, qw_ref, q_ref, kiw_ref, k_ref, vt_ref, o_ref,
                s_scr, qs_scr, *acc_scrs, tq, tk, topk, t0, t_more, hg):
    i = pl.program_id(1)
    nkb = (i + 1) * (tq // tk)
    q0 = i * tq
    seq = s_scr.shape[0]
    nsub = tk // SUBLANES

    key_t = lax.broadcasted_iota(jnp.int32, (tk, tq), 0)
    qry_t = q0 + lax.broadcasted_iota(jnp.int32, (tk, tq), 1)
    qchunk_t = qry_t // CHUNK
    qcol = q0 + lax.broadcasted_iota(jnp.int32, (1, tq), 1)
    kk = jnp.minimum((qcol // CHUNK + 1) * CHUNK, topk).astype(F32)

    w_t = (jnp.transpose(qw_ref[...].astype(F32))[IDX_DIM:IDX_DIM + IDX_HEADS, :]
           * (IDX_HEADS ** -0.5 * IDX_DIM ** -0.5))

    def tree(op, x):
        while x.shape[0] > 1:
            half = x.shape[0] // 2
            x = op(x[:half], x[half:])
        return x[0]

    def by_vreg(x):
        return x.reshape(nsub, SUBLANES, tq)

    def score_body(kb, carry):
        mn8, mx8 = carry
        k0 = pl.multiple_of(kb * tk, tk)
        kblk = kiw_ref[pl.ds(k0, tk), :][:, :IDX_DIM]
        ds = [_nt_dot(kblk, qi_ref[:, h * IDX_DIM:(h + 1) * IDX_DIM]) for h in range(IDX_HEADS)]
        acc = w_t[0:1, :] * jnp.maximum(ds[0], 0.0)
        for h in range(1, IDX_HEADS):
            acc = acc + w_t[h:h + 1, :] * jnp.maximum(ds[h], 0.0)
        adm = (k0 + key_t) // CHUNK <= qchunk_t
        s_scr[pl.ds(k0, tk), :] = jnp.where(adm, acc, -jnp.inf)
        mn8 = jnp.minimum(mn8, tree(jnp.minimum, by_vreg(jnp.where(adm, acc, jnp.inf))))
        mx8 = jnp.maximum(mx8, tree(jnp.maximum, by_vreg(jnp.where(adm, acc, -jnp.inf))))
        return mn8, mx8

    mn8, mx8 = lax.fori_loop(0, nkb, score_body, (jnp.full((SUBLANES, tq), jnp.inf, F32),
                                                  jnp.full((SUBLANES, tq), -jnp.inf, F32)))
    mn = jnp.min(mn8, axis=0, keepdims=True)
    mx = jnp.max(mx8, axis=0, keepdims=True)

    def fold(fn, init):
        def body(kb, c):
            k0 = pl.multiple_of(kb * tk, tk)
            return fn(c, s_scr[pl.ds(k0, tk), :], k0)
        return lax.fori_loop(0, nkb, body, init)

    def count(pred_fn):
        c = fold(lambda c, blk, k0: c + tree(jnp.add, by_vreg(jnp.where(pred_fn(blk, k0), 1.0, 0.0))),
                 jnp.zeros((SUBLANES, tq), F32))
        return jnp.sum(c, axis=0, keepdims=True)

    def any_lane(pred):
        return jnp.max(jnp.where(pred, 1.0, 0.0)) > 0.0

    def bisect(n_it, lo, hi, c_lo):
        def it(_, st):
            lo, hi, c_lo = st
            mid = 0.5 * lo + 0.5 * hi
            c = count(lambda blk, k0: blk >= mid)
            ok = c >= kk
            return jnp.where(ok, mid, lo), jnp.where(ok, hi, mid), jnp.where(ok, c, c_lo)
        return lax.fori_loop(0, n_it, it, (lo, hi, c_lo))

    def b_cond(st):
        lo, hi, c_lo, it = st
        return jnp.logical_and(any_lane(c_lo != kk), it < t0)

    def b_body(st):
        lo, hi, c_lo, it = st
        lo, hi, c_lo = bisect(2, lo, hi, c_lo)
        return lo, hi, c_lo, it + 2

    n_adm = ((qcol // CHUNK + 1) * CHUNK).astype(F32)
    lo, hi, c_lo, _ = lax.while_loop(b_cond, b_body, (mn, mx, n_adm, jnp.int32(0)))
    big_cut = jnp.full((1, tq), 2 * seq, jnp.int32)

    def exact_ties():
        def fold_min(sel_fn):
            c = fold(lambda c, blk, k0: jnp.minimum(c, tree(jnp.minimum, by_vreg(sel_fn(blk)))),
                     jnp.full((SUBLANES, tq), jnp.inf, F32))
            return jnp.min(c, axis=0, keepdims=True)

        def verify(lo):
            thr = fold_min(lambda blk: jnp.where(blk >= lo, blk, jnp.inf))
            return thr, count(lambda blk, k0: blk > thr)

        def w_cond(st):
            lo, hi, c_lo, thr, n_gt, it = st
            return jnp.logical_and(any_lane(n_gt >= kk), it < 64)

        def w_body(st):
            lo, hi, c_lo, thr, n_gt, it = st
            lo, hi, c_lo = bisect(t_more, lo, hi, c_lo)
            thr, n_gt = verify(lo)
            return lo, hi, c_lo, thr, n_gt, it + 1

        thr, n_gt = verify(lo)
        _, _, _, thr, n_gt, _ = lax.while_loop(w_cond, w_body, (lo, hi, c_lo, thr, n_gt, jnp.int32(0)))
        need = kk - n_gt
        nbits = (2 * seq - 1).bit_length()

        def it(b, c):
            cand = c + jnp.left_shift(jnp.int32(1), nbits - 1 - b)
            g_c = count(lambda blk, k0: jnp.logical_and(blk == thr, (k0 + key_t) < cand))
            return jnp.where(g_c <= need, cand, c)
        return thr, lax.fori_loop(0, nbits, it, jnp.zeros((1, tq), jnp.int32))

    thr, cut = lax.cond(any_lane(c_lo != kk), exact_ties, lambda: (lo, big_cut))

    def mask_body(kb, carry):
        k0 = pl.multiple_of(kb * tk, tk)
        blk = s_scr[pl.ds(k0, tk), :]
        sel = jnp.logical_or(blk > thr, jnp.logical_and(blk == thr, (k0 + key_t) < cut))
        s_scr[pl.ds(k0, tk), :] = jnp.where(sel, 0.0, MASK_DIST)
        return carry

    lax.fori_loop(0, nkb, mask_body, 0)

    log2e = math.log2(math.e)
    qs_scr[...] = (q_ref[...].astype(F32) * (SA_HEAD_DIM ** -0.5 * log2e)).astype(BF16)
    for acc_scr in acc_scrs:
        acc_scr[...] = jnp.zeros(acc_scr.shape, F32)

    def att_body(kb, carry):
        m_all, l_all = carry
        k0 = pl.multiple_of(kb * tk, tk)
        dist = jnp.abs(qry_t - (k0 + key_t)).astype(F32) + s_scr[pl.ds(k0, tk), :]
        cols = [slice(h * SA_HEAD_DIM, (h + 1) * SA_HEAD_DIM) for h in range(SA_HEADS)]
        m_rows, l_rows = [], []
        for g0 in range(0, SA_HEADS, hg):
            heads = range(g0, g0 + hg)
            qk = {h: _nt_dot(k_ref[pl.ds(k0, tk), cols[h]], qs_scr[:, cols[h]]) for h in heads}
            alphas, ps = {}, {}
            for h in heads:
                slope2 = 2.0 ** (-8.0 * (h + 1) / SA_HEADS) * log2e
                logits = qk[h] - slope2 * dist
                m_old = m_all[h:h + 1, :]
                m_new = jnp.maximum(m_old, jnp.max(logits, axis=0, keepdims=True))
                alphas[h] = jnp.exp2(m_old - m_new)
                p = jnp.exp2(logits - m_new)
                l_rows.append(alphas[h] * l_all[h:h + 1, :] + jnp.sum(p, axis=0, keepdims=True))
                m_rows.append(m_new)
                ps[h] = p.astype(BF16)
            pv = {h: jnp.dot(vt_ref[cols[h], pl.ds(k0, tk)], ps[h], preferred_element_type=F32) for h in heads}
            for h in heads:
                acc_scrs[h][...] = alphas[h] * acc_scrs[h][...] + pv[h]
        return jnp.concatenate(m_rows, axis=0), jnp.concatenate(l_rows, axis=0)

    _, l_all = lax.fori_loop(0, nkb, att_body,
                             (jnp.full((SA_HEADS, tq), M_INIT, F32), jnp.zeros((SA_HEADS, tq), F32)))

    for h in range(SA_HEADS):
        o_ref[:, h * SA_HEAD_DIM:(h + 1) * SA_HEAD_DIM] = jnp.transpose(acc_scrs[h][...] / l_all[h:h + 1, :])


def _dsa(p16, v_t, bsz, seq, tq, tk):
    n = bsz * seq
    steps = seq // tq
    topk = min(MAX_TOPK, seq // 4)
    once = dict(pipeline_mode=pl.Buffered(1))
    qrow = lambda cb: (lambda b, i: (b * steps + i, cb))
    return pl.pallas_call(
        functools.partial(_dsa_kernel, tq=tq, tk=tk, topk=topk, t0=28, t_more=4, hg=SA_HEADS),
        grid=(bsz, steps),
        in_specs=[
            pl.BlockSpec((tq, IDX_WIDTH), qrow(O16_IQ // IDX_WIDTH)),
            pl.BlockSpec((tq, LANES), qrow(O16_IKW // LANES)),
            pl.BlockSpec((tq, SA_WIDTH), qrow(O16_Q // SA_WIDTH)),
            pl.BlockSpec((seq, LANES), lambda b, i: (b, O16_IKW // LANES), **once),
            pl.BlockSpec((seq, SA_WIDTH), lambda b, i: (b, O16_K // SA_WIDTH), **once),
            pl.BlockSpec((SA_WIDTH, seq), lambda b, i: (b, 0), **once),
        ],
        out_specs=pl.BlockSpec((tq, SA_WIDTH), lambda b, i: (b * steps + i, 0)),
        out_shape=jax.ShapeDtypeStruct((n, SA_WIDTH), F32),
        scratch_shapes=[
            pltpu.VMEM((seq, tq), F32),
            pltpu.VMEM((tq, SA_WIDTH), BF16),
        ] + [pltpu.VMEM((SA_HEAD_DIM, tq), F32) for _ in range(SA_HEADS)],
        name="dsa",
        compiler_params=pltpu.CompilerParams(
            dimension_semantics=("parallel", "arbitrary"), vmem_limit_bytes=VMEM_LIMIT),
    )(p16, p16, p16, p16, p16, v_t)


def _layer_norm(x, g, b):
    mu = jnp.mean(x, axis=-1, keepdims=True)
    xc = x - mu
    var = jnp.mean(xc * xc, axis=-1, keepdims=True)
    return xc * lax.rsqrt(var + LN_EPS) * g + b


def _merge_kernel(a_ref, b_ref, ga_ref, gb_ref, x_ref, wa_ref, wb_ref, wo_ref, g1_ref, b1_ref,
                  wrh_ref, wrl_ref, br_ref, h_ref, ids_ref, wts_ref):
    merged = (_sigmoid(ga_ref[...]) * jnp.dot(a_ref[...].astype(BF16), wa_ref[...], preferred_element_type=F32)
              + _sigmoid(gb_ref[...]) * jnp.dot(b_ref[...].astype(BF16), wb_ref[...], preferred_element_type=F32))
    mix = jnp.dot(merged.astype(BF16), wo_ref[...], preferred_element_type=F32)
    h = _layer_norm(DN_ALPHA * x_ref[...] + mix, g1_ref[...], b1_ref[...])
    h_ref[...] = h

    h_hi = h.astype(BF16)
    h_lo = (h - h_hi.astype(F32)).astype(BF16)
    logits = (jnp.dot(h_hi, wrh_ref[...], preferred_element_type=F32)
              + jnp.dot(h_lo, wrh_ref[...], preferred_element_type=F32)
              + jnp.dot(h_hi, wrl_ref[...], preferred_element_type=F32)) + br_ref[...]
    lane = lax.broadcasted_iota(jnp.int32, logits.shape, 1)
    big = jnp.int32(4 * LANES)

    def first_argmax(vals):
        mval = jnp.max(vals, axis=-1, keepdims=True)
        idx = jnp.min(jnp.where(vals == mval, lane, big), axis=-1, keepdims=True)
        return mval, idx

    gl = jnp.where(lane < N_GROUPS, logits, -jnp.inf)
    g_max, g_idx = first_argmax(gl)
    g_top_p = 1.0 / jnp.sum(jnp.exp(gl - g_max), axis=-1, keepdims=True)
    e_lo = N_GROUPS + g_idx * EXPERTS_PER_GROUP
    el = jnp.where(jnp.logical_and(lane >= e_lo, lane < e_lo + EXPERTS_PER_GROUP), logits, -jnp.inf)
    e1, i1 = first_argmax(el)
    e2, i2 = first_argmax(jnp.where(lane == i1, -jnp.inf, el))
    t = jnp.exp(e2 - e1)
    w1 = (1.0 / (1.0 + t)) * g_top_p
    w2 = (t / (1.0 + t)) * g_top_p
    ids_ref[...] = jnp.where(lane == 0, i1 - N_GROUPS, jnp.where(lane == 1, i2 - N_GROUPS, 0))
    wts_ref[...] = jnp.where(lane == 0, w1, jnp.where(lane == 1, w2, 0.0))


def _merge(branch_a, branch_b, p32, x2, wa, wb, wo, g1, b1, wr_hi, wr_lo, br, tm):
    n, d = x2.shape
    once = dict(pipeline_mode=pl.Buffered(1))
    ga_blk = O32_GATES // d
    row = lambda i: (i, 0)
    const = lambda i: (0, 0)
    return pl.pallas_call(
        _merge_kernel,
        grid=(n // tm,),
        in_specs=[
            pl.BlockSpec((tm, HG_WIDTH), row),
            pl.BlockSpec((tm, SA_WIDTH), row),
            pl.BlockSpec((tm, d), lambda i: (i, ga_blk)),
            pl.BlockSpec((tm, d), lambda i: (i, ga_blk + 1)),
            pl.BlockSpec((tm, d), row),
            pl.BlockSpec((HG_WIDTH, d), const, **once),
            pl.BlockSpec((SA_WIDTH, d), const, **once),
            pl.BlockSpec((d, d), const, **once),
            pl.BlockSpec((1, d), const),
            pl.BlockSpec((1, d), const),
            pl.BlockSpec((d, LANES), const, **once),
            pl.BlockSpec((d, LANES), const, **once),
            pl.BlockSpec((1, LANES), const),
        ],
        out_specs=[pl.BlockSpec((tm, d), row), pl.BlockSpec((tm, LANES), row), pl.BlockSpec((tm, LANES), row)],
        out_shape=[jax.ShapeDtypeStruct((n, d), F32),
                   jax.ShapeDtypeStruct((n, LANES), jnp.int32),
                   jax.ShapeDtypeStruct((n, LANES), F32)],
        name="merge_ln1_router",
        compiler_params=pltpu.CompilerParams(
            dimension_semantics=("parallel",), vmem_limit_bytes=VMEM_LIMIT),
    )(branch_a, branch_b, p32, p32, x2, wa, wb, wo, g1, b1, wr_hi, wr_lo, br)


def _row_gather(idx_ref, base, n_rows, src_hbm, dst, sem):
    for r in range(n_rows):
        pltpu.make_async_copy(src_hbm.at[pl.ds(idx_ref[base + r], 1)], dst.at[pl.ds(r, 1)], sem).start()


def _row_gather_wait(n_rows, src_hbm, dst, sem):
    pltpu.make_async_copy(src_hbm.at[pl.ds(0, n_rows)], dst, sem).wait()


def _moe_up_kernel(be_ref, nu_ref, rt_ref, h_hbm, wg_ref, wu_ref, o_ref, xbuf, sem, *, bm, n_blocks):
    i = pl.program_id(0)
    nu = nu_ref[0]
    slot = i % 2

    @pl.when(i == 0)
    def _():
        _row_gather(rt_ref, 0, bm, h_hbm, xbuf.at[0], sem.at[0])

    @pl.when(i < nu)
    def _():
        nxt = jnp.minimum(i + 1, n_blocks - 1)
        _row_gather(rt_ref, nxt * bm, bm, h_hbm, xbuf.at[1 - slot], sem.at[1 - slot])
        _row_gather_wait(bm, h_hbm, xbuf.at[slot], sem.at[slot])
        xb = xbuf[slot].astype(BF16)
        g = jnp.dot(xb, wg_ref[0], preferred_element_type=F32)
        u = jnp.dot(xb, wu_ref[0], preferred_element_type=F32)
        o_ref[...] = (_silu(g) * u).astype(BF16)

        @pl.when(i == n_blocks - 1)
        def _():
            _row_gather_wait(bm, h_hbm, xbuf.at[1 - slot], sem.at[1 - slot])

    @pl.when(i >= nu)
    def _():
        @pl.when(i == nu)
        def _():
            _row_gather_wait(bm, h_hbm, xbuf.at[slot], sem.at[slot])
        o_ref[...] = jnp.zeros_like(o_ref)


def _moe_down_kernel(be_ref, nu_ref, h_ref, wd_ref, o_ref):
    @pl.when(pl.program_id(0) < nu_ref[0])
    def _():
        o_ref[...] = jnp.dot(h_ref[...], wd_ref[0], preferred_element_type=F32)

    @pl.when(pl.program_id(0) >= nu_ref[0])
    def _():
        o_ref[...] = jnp.zeros_like(o_ref)


def _moe_ffn(h1, row_tok, block_e, n_used, wg, wu, wd, bm):
    n_rows = row_tok.shape[0]
    d = h1.shape[1]
    f = wg.shape[2]
    n_blocks = n_rows // bm
    params = pltpu.CompilerParams(dimension_semantics=("arbitrary",), vmem_limit_bytes=VMEM_LIMIT)
    hb = pl.pallas_call(
        functools.partial(_moe_up_kernel, bm=bm, n_blocks=n_blocks),
        grid_spec=pltpu.PrefetchScalarGridSpec(
            num_scalar_prefetch=3, grid=(n_blocks,),
            in_specs=[pl.BlockSpec(memory_space=pl.ANY),
                      pl.BlockSpec((1, d, f), lambda i, be, nu, rt: (be[i], 0, 0)),
                      pl.BlockSpec((1, d, f), lambda i, be, nu, rt: (be[i], 0, 0))],
            out_specs=pl.BlockSpec((bm, f), lambda i, be, nu, rt: (i, 0)),
            scratch_shapes=[pltpu.VMEM((2, bm, d), F32), pltpu.SemaphoreType.DMA((2,))]),
        out_shape=jax.ShapeDtypeStruct((n_rows, f), BF16),
        name="moe_up", compiler_params=params,
    )(block_e, n_used, row_tok, h1, wg, wu)
    return pl.pallas_call(
        _moe_down_kernel,
        grid_spec=pltpu.PrefetchScalarGridSpec(
            num_scalar_prefetch=2, grid=(n_blocks,),
            in_specs=[pl.BlockSpec((bm, f), lambda i, be, nu: (i, 0)),
                      pl.BlockSpec((1, f, d), lambda i, be, nu: (be[i], 0, 0))],
            out_specs=pl.BlockSpec((bm, d), lambda i, be, nu: (i, 0))),
        out_shape=jax.ShapeDtypeStruct((n_rows, d), F32),
        name="moe_down", compiler_params=params,
    )(block_e, n_used, hb, wd)


def _combine_kernel(dest_ref, ys_hbm, h_ref, wts_ref, g2_ref, b2_ref, o_ref, ybuf, sem, *, tm, n_steps):
    i = pl.program_id(0)
    slot = i % 2

    @pl.when(i == 0)
    def _():
        _row_gather(dest_ref, 0, 2 * tm, ys_hbm, ybuf.at[0], sem.at[0])

    nxt = jnp.minimum(i + 1, n_steps - 1)
    _row_gather(dest_ref, nxt * 2 * tm, 2 * tm, ys_hbm, ybuf.at[1 - slot], sem.at[1 - slot])
    _row_gather_wait(2 * tm, ys_hbm, ybuf.at[slot], sem.at[slot])
    w0 = wts_ref[:, 0:1]
    w1 = wts_ref[:, 1:2]
    ffn = w0 * ybuf[slot, 0:tm, :] + w1 * ybuf[slot, tm:2 * tm, :]
    o_ref[...] = _layer_norm(DN_ALPHA * h_ref[...] + ffn, g2_ref[...], b2_ref[...])

    @pl.when(i == n_steps - 1)
    def _():
        _row_gather_wait(2 * tm, ys_hbm, ybuf.at[1 - slot], sem.at[1 - slot])


def _combine(h1, ys, dest_tiles, wts, g2, b2, tm):
    n, d = h1.shape
    n_steps = n // tm
    row = lambda i, dst: (i, 0)
    const = lambda i, dst: (0, 0)
    return pl.pallas_call(
        functools.partial(_combine_kernel, tm=tm, n_steps=n_steps),
        grid_spec=pltpu.PrefetchScalarGridSpec(
            num_scalar_prefetch=1, grid=(n_steps,),
            in_specs=[pl.BlockSpec(memory_space=pl.ANY), pl.BlockSpec((tm, d), row),
                      pl.BlockSpec((tm, LANES), row), pl.BlockSpec((1, d), const), pl.BlockSpec((1, d), const)],
            out_specs=pl.BlockSpec((tm, d), row),
            scratch_shapes=[pltpu.VMEM((2, 2 * tm, d), F32), pltpu.SemaphoreType.DMA((2,))]),
        out_shape=jax.ShapeDtypeStruct((n, d), F32),
        name="combine_ln2",
        compiler_params=pltpu.CompilerParams(
            dimension_semantics=("arbitrary",), vmem_limit_bytes=VMEM_LIMIT),
    )(dest_tiles, ys, h1, wts, g2, b2)


def _dispatch_plan(e_ids, bm):
    n_tok = e_ids.shape[0]
    n_assign = n_tok * TOPK_IN_GROUP
    e = e_ids.reshape(-1)
    onehot = (e[:, None] == jnp.arange(N_EXPERTS, dtype=jnp.int32)[None, :]).astype(jnp.int32)
    cum = jnp.cumsum(onehot, axis=0)
    rank = jnp.take_along_axis(cum, e[:, None], axis=1)[:, 0] - 1
    counts = cum[-1]
    padded = (counts + bm - 1) // bm * bm
    pend = jnp.cumsum(padded)
    pstart = pend - padded
    dest = pstart[e] + rank
    n_blocks = -(-(n_assign + N_EXPERTS * (bm - 1)) // bm)
    n_rows = n_blocks * bm
    tok = jnp.repeat(jnp.arange(n_tok, dtype=jnp.int32), TOPK_IN_GROUP)
    row_tok = jnp.zeros((n_rows,), jnp.int32).at[dest].set(tok)
    block_pos = jnp.arange(n_blocks, dtype=jnp.int32) * bm
    block_e = jnp.sum((pend[None, :] <= block_pos[:, None]).astype(jnp.int32), axis=1)
    block_e = jnp.minimum(block_e, N_EXPERTS - 1)
    n_used = (pend[-1] // bm).astype(jnp.int32).reshape(1)
    return row_tok, dest.reshape(n_tok, TOPK_IN_GROUP), block_e, n_used


def kernel(x, w_in, b_in, hg_lb_logits, hg_norm_g, w_branch_a, w_branch_b, w_out, ln1_g, ln1_b,
           w_group, b_group, w_router, b_router, w_gate, w_up, w_down, ln2_g, ln2_b):
    bsz, seq, d = x.shape
    n = bsz * seq
    x2 = x.reshape(n, d)
    l = 0

    def tail(src):
        kw = src[..., MAIN_COLS:MAIN_COLS + IDX_DIM + IDX_HEADS]
        pad = jnp.zeros(src.shape[:-1] + (PROJ_TN - IDX_DIM - IDX_HEADS,), src.dtype)
        return jnp.concatenate([kw, pad, src[..., MAIN_COLS + IDX_DIM + IDX_HEADS:]], axis=-1)

    p32, p16 = _in_proj(x2, w_in[l, :, :MAIN_COLS].astype(BF16), b_in[l].reshape(1, -1),
                        tail(w_in[l]).astype(BF16), tail(b_in[l]).reshape(1, -1), tm=min(1024, n))

    branch_a = _hgrn(p32, hg_lb_logits, hg_norm_g[l].reshape(1, HG_WIDTH), bsz, seq, min(512, seq))

    v_t = p16[:, O16_V:O16_V + SA_WIDTH].reshape(bsz, seq, SA_WIDTH).transpose(0, 2, 1).reshape(bsz * SA_WIDTH, seq)
    tq = min(256, seq)
    branch_b = _dsa(p16, v_t, bsz, seq, tq=tq, tk=min(256, tq))

    wr = jnp.zeros((d, LANES), F32)
    wr = wr.at[:, :N_GROUPS].set(w_group[l]).at[:, N_GROUPS:N_GROUPS + N_EXPERTS].set(w_router[l])
    wr_hi = wr.astype(BF16)
    wr_lo = (wr - wr_hi.astype(F32)).astype(BF16)
    br = jnp.zeros((1, LANES), F32)
    br = br.at[0, :N_GROUPS].set(b_group[l]).at[0, N_GROUPS:N_GROUPS + N_EXPERTS].set(b_router[l])
    tm = min(256, n)
    h1, ids, wts = _merge(branch_a, branch_b, p32, x2,
                          w_branch_a[l].astype(BF16), w_branch_b[l].astype(BF16), w_out[l].astype(BF16),
                          ln1_g[l].reshape(1, d), ln1_b[l].reshape(1, d), wr_hi, wr_lo, br, tm)

    bm = 256
    row_tok, dest, block_e, n_used = _dispatch_plan(ids[:, :TOPK_IN_GROUP], bm)
    ys = _moe_ffn(h1, row_tok, block_e, n_used, w_gate[l], w_up[l], w_down[l], bm)
    dest_tiles = dest.reshape(n // tm, tm, TOPK_IN_GROUP).transpose(0, 2, 1).reshape(-1)
    out = _combine(h1, ys, dest_tiles, wts, ln2_g[l].reshape(1, d), ln2_b[l].reshape(1, d), tm)
    return out.reshape(bsz, seq, d)
```

```python
import functools
import math

import jax
import jax.numpy as jnp
from jax import lax
from jax.experimental import pallas as pl
from jax.experimental.pallas import tpu as pltpu

CHUNK = 64
HG_HEADS = 8
HG_DK = 128
HG_DV = 128
HG_WIDTH = HG_HEADS * HG_DK
SA_HEADS = 8
SA_HEAD_DIM = 128
SA_WIDTH = SA_HEADS * SA_HEAD_DIM
IDX_HEADS = 8
IDX_DIM = 64
IDX_WIDTH = IDX_HEADS * IDX_DIM
MAX_TOPK = 256
N_GROUPS = 4
EXPERTS_PER_GROUP = 8
N_EXPERTS = N_GROUPS * EXPERTS_PER_GROUP
TOPK_IN_GROUP = 2
DEPTH = 1
DN_ALPHA = (2.0 * DEPTH) ** 0.25
LN_EPS = 1e-5
RMS_EPS = 1e-6

LANES = 128
SUBLANES = 8
VMEM_LIMIT = 56 * 1024 * 1024
MASK_DIST = 1e30
M_INIT = -1e30

F32 = jnp.float32
BF16 = jnp.bfloat16

MAIN_COLS = 4 * HG_WIDTH + 3 * SA_WIDTH + IDX_WIDTH
IKW_COLS = 512
O16_LO = 4 * HG_WIDTH
O16_COLS = 3 * SA_WIDTH + IDX_WIDTH + IKW_COLS
O16_Q, O16_K, O16_V, O16_IQ, O16_IKW = 0, SA_WIDTH, 2 * SA_WIDTH, 3 * SA_WIDTH, 3 * SA_WIDTH + IDX_WIDTH
O32_GATES = 4 * HG_WIDTH


def _nt_dot(a, b):
    return lax.dot_general(a, b, (((1,), (1,)), ((), ())), preferred_element_type=F32)


def _tn_dot(a, b):
    return lax.dot_general(a, b, (((0,), (0,)), ((), ())), preferred_element_type=F32)


def _sigmoid(x):
    return 1.0 / (1.0 + jnp.exp(-x))


def _silu(x):
    return x * _sigmoid(x)


def _proj_kernel(x_ref, w_ref, b_ref, o32_ref, o16_ref, xb_ref, *, j16_lo, j16_hi):
    j = pl.program_id(1)

    @pl.when(j == 0)
    def _():
        xb_ref[...] = x_ref[...].astype(BF16)

    def block():
        return jnp.dot(xb_ref[...], w_ref[...], preferred_element_type=F32) + b_ref[...]

    in16 = jnp.logical_and(j >= j16_lo, j < j16_hi)

    @pl.when(in16)
    def _():
        o16_ref[...] = block().astype(BF16)

    @pl.when(jnp.logical_not(in16))
    def _():
        o32_ref[...] = block()


def _in_proj(x2, w_all, b_all, tm, tn):
    n, d = x2.shape
    cols = w_all.shape[1]
    nj = cols // tn
    lo, hi = O16_LO // tn, (O16_LO + O16_COLS) // tn
    return pl.pallas_call(
        functools.partial(_proj_kernel, j16_lo=lo, j16_hi=hi),
        grid=(n // tm, nj),
        in_specs=[
            pl.BlockSpec((tm, d), lambda i, j: (i, 0)),
            pl.BlockSpec((d, tn), lambda i, j: (0, j)),
            pl.BlockSpec((1, tn), lambda i, j: (0, j)),
        ],
        out_specs=[
            pl.BlockSpec((tm, tn), lambda i, j: (i, jnp.where(j < lo, j, jnp.maximum(j, hi) - (hi - lo)))),
            pl.BlockSpec((tm, tn), lambda i, j: (i, jnp.clip(j - lo, 0, hi - lo - 1))),
        ],
        out_shape=[jax.ShapeDtypeStruct((n, cols - O16_COLS), F32), jax.ShapeDtypeStruct((n, O16_COLS), BF16)],
        scratch_shapes=[pltpu.VMEM((tm, d), BF16)],
        name="in_proj",
        compiler_params=pltpu.CompilerParams(
            dimension_semantics=("parallel", "arbitrary"), vmem_limit_bytes=VMEM_LIMIT),
    )(x2, w_all, b_all)


def _hgrn_kernel(q_ref, f_ref, i_ref, g_ref, lbl_ref, ng_ref, o_ref, st_ref, *, n_chunks):
    @pl.when(pl.program_id(1) == 0)
    def _():
        st_ref[...] = jnp.zeros_like(st_ref)

    lbl = lbl_ref[...].astype(F32)
    lbe = jnp.exp(lbl - jnp.max(lbl, axis=0, keepdims=True))
    lb = lbe[0:1, :] / jnp.sum(lbe, axis=0, keepdims=True)

    r_i = lax.broadcasted_iota(jnp.int32, (CHUNK, CHUNK), 0)
    c_i = lax.broadcasted_iota(jnp.int32, (CHUNK, CHUNK), 1)
    causal = c_i <= r_i
    tri = jnp.where(causal, 1.0, 0.0).astype(BF16)

    def chunk_body(c, carry):
        r0 = pl.multiple_of(c * CHUNK, CHUNK)
        rows = pl.ds(r0, CHUNK)
        heads = range(HG_HEADS)
        cols = [slice(h * HG_DK, (h + 1) * HG_DK) for h in heads]
        f = [lb[:, cols[h]] + (1.0 - lb[:, cols[h]]) * _sigmoid(f_ref[rows, cols[h]]) for h in heads]
        logf = [jnp.log(f[h]) for h in heads]
        hi = [logf[h].astype(BF16) for h in heads]
        lo = [(logf[h] - hi[h].astype(F32)).astype(BF16) for h in heads]
        cum = [jnp.dot(tri, hi[h], preferred_element_type=F32) + jnp.dot(tri, lo[h], preferred_element_type=F32)
               for h in heads]
        cum_last = [cum[h][CHUNK - 1:CHUNK, :] for h in heads]
        vh = [i_ref[rows, cols[h]].astype(BF16) for h in heads]
        q_dec = [(_silu(q_ref[rows, cols[h]]) * jnp.exp(cum[h])).astype(BF16) for h in heads]
        k_dec = [((1.0 - f[h]) * jnp.exp(-cum[h])).astype(BF16) for h in heads]
        k_end = [((1.0 - f[h]) * jnp.exp(cum_last[h] - cum[h])).astype(BF16) for h in heads]
        st = [st_ref[h] for h in heads]
        att = [jnp.where(causal, _nt_dot(q_dec[h], k_dec[h]), 0.0).astype(BF16) for h in heads]
        o_st = [_nt_dot(q_dec[h], st[h].astype(BF16)) for h in heads]
        kv = [_tn_dot(vh[h], k_end[h]) for h in heads]
        o = [jnp.dot(att[h], vh[h], preferred_element_type=F32) + o_st[h] for h in heads]
        for h in heads:
            st_ref[h] = st[h] * jnp.exp(cum_last[h]) + kv[h]
            on = o[h] * lax.rsqrt(jnp.mean(o[h] * o[h], axis=-1, keepdims=True) + RMS_EPS)
            o_ref[rows, cols[h]] = on * ng_ref[:, cols[h]] * _silu(g_ref[rows, cols[h]])
        return carry

    lax.fori_loop(0, n_chunks, chunk_body, 0)


def _hgrn(p32, lb_logits, norm_g, bsz, seq, rows_per_step):
    n = bsz * seq
    w = HG_WIDTH
    steps = seq // rows_per_step
    nl = lb_logits.shape[0]

    def col_spec(cb):
        return pl.BlockSpec((rows_per_step, w), lambda b, s: (b * steps + s, cb))

    return pl.pallas_call(
        functools.partial(_hgrn_kernel, n_chunks=rows_per_step // CHUNK),
        grid=(bsz, steps),
        in_specs=[col_spec(0), col_spec(1), col_spec(2), col_spec(3),
                  pl.BlockSpec((nl, w), lambda b, s: (0, 0)),
                  pl.BlockSpec((1, w), lambda b, s: (0, 0))],
        out_specs=pl.BlockSpec((rows_per_step, w), lambda b, s: (b * steps + s, 0)),
        out_shape=jax.ShapeDtypeStruct((n, w), F32),
        scratch_shapes=[pltpu.VMEM((HG_HEADS, HG_DV, HG_DK), F32)],
        name="hgrn",
        compiler_params=pltpu.CompilerParams(
            dimension_semantics=("parallel", "arbitrary"), vmem_limit_bytes=VMEM_LIMIT),
    )(p32, p32, p32, p32, lb_logits, norm_g)


def _dsa_kernel(qi_ref, qw_ref, q_ref, kiw_ref, k_ref, vt_ref, o_ref,
                s_scr, qs_scr, *acc_scrs, tq, tk, topk, t0, t_more, hg, kpb):
    i = pl.program_id(1)
    nkb = (i + 1) * (tq // tk)
    q0 = i * tq
    seq = s_scr.shape[0]
    nsub = tk // SUBLANES

    key_t = lax.broadcasted_iota(jnp.int32, (tk, tq), 0)
    qry_t = q0 + lax.broadcasted_iota(jnp.int32, (tk, tq), 1)
    qchunk_t = qry_t // CHUNK
    qcol = q0 + lax.broadcasted_iota(jnp.int32, (1, tq), 1)
    kk = jnp.minimum((qcol // CHUNK + 1) * CHUNK, topk).astype(F32)

    w_t = (jnp.transpose(qw_ref[...].astype(F32))[IDX_DIM:IDX_DIM + IDX_HEADS, :]
           * (IDX_HEADS ** -0.5 * IDX_DIM ** -0.5))

    def tree(op, x):
        while x.shape[0] > 1:
            half = x.shape[0] // 2
            x = op(x[:half], x[half:])
        return x[0]

    def by_vreg(x):
        return x.reshape(nsub, SUBLANES, tq)

    def score_body(kb, carry):
        mn8, mx8 = carry
        k0 = pl.multiple_of(kb * tk, tk)
        kblk = kiw_ref[pl.ds(k0, tk), :][:, :IDX_DIM]
        ds = [_nt_dot(kblk, qi_ref[:, h * IDX_DIM:(h + 1) * IDX_DIM]) for h in range(IDX_HEADS)]
        acc = w_t[0:1, :] * jnp.maximum(ds[0], 0.0)
        for h in range(1, IDX_HEADS):
            acc = acc + w_t[h:h + 1, :] * jnp.maximum(ds[h], 0.0)
        adm = (k0 + key_t) // CHUNK <= qchunk_t
        s_scr[pl.ds(k0, tk), :] = jnp.where(adm, acc, -jnp.inf)
        mn8 = jnp.minimum(mn8, tree(jnp.minimum, by_vreg(jnp.where(adm, acc, jnp.inf))))
        mx8 = jnp.maximum(mx8, tree(jnp.maximum, by_vreg(jnp.where(adm, acc, -jnp.inf))))
        return mn8, mx8

    mn8, mx8 = lax.fori_loop(0, nkb, score_body, (jnp.full((SUBLANES, tq), jnp.inf, F32),
                                                  jnp.full((SUBLANES, tq), -jnp.inf, F32)))
    mn = jnp.min(mn8, axis=0, keepdims=True)
    mx = jnp.max(mx8, axis=0, keepdims=True)

    def fold(fn, init):
        def body(kb, c):
            k0 = pl.multiple_of(kb * tk, tk)
            return fn(c, s_scr[pl.ds(k0, tk), :], k0)
        return lax.fori_loop(0, nkb, body, init)

    def count(pred_fn):
        c = fold(lambda c, blk, k0: c + tree(jnp.add, by_vreg(jnp.where(pred_fn(blk, k0), 1.0, 0.0))),
                 jnp.zeros((SUBLANES, tq), F32))
        return jnp.sum(c, axis=0, keepdims=True)

    def any_lane(pred):
        return jnp.max(jnp.where(pred, 1.0, 0.0)) > 0.0

    def bisect(n_it, lo, hi, c_lo):
        def it(_, st):
            lo, hi, c_lo = st
            mid = 0.5 * lo + 0.5 * hi
            c = count(lambda blk, k0: blk >= mid)
            ok = c >= kk
            return jnp.where(ok, mid, lo), jnp.where(ok, hi, mid), jnp.where(ok, c, c_lo)
        return lax.fori_loop(0, n_it, it, (lo, hi, c_lo))

    def b_cond(st):
        lo, hi, c_lo, it = st
        return jnp.logical_and(any_lane(c_lo != kk), it < t0)

    def b_body(st):
        lo, hi, c_lo, it = st
        lo, hi, c_lo = bisect(2, lo, hi, c_lo)
        return lo, hi, c_lo, it + 2

    n_adm = ((qcol // CHUNK + 1) * CHUNK).astype(F32)
    lo, hi, c_lo, _ = lax.while_loop(b_cond, b_body, (mn, mx, n_adm, jnp.int32(0)))

    def penalty_plain():
        def body(kb, carry):
            k0 = pl.multiple_of(kb * tk, tk)
            s_scr[pl.ds(k0, tk), :] = jnp.where(s_scr[pl.ds(k0, tk), :] >= lo, 0.0, MASK_DIST)
            return carry
        lax.fori_loop(0, nkb, body, 0)

    def penalty_with_ties():
        def fold_min(sel_fn):
            c = fold(lambda c, blk, k0: jnp.minimum(c, tree(jnp.minimum, by_vreg(sel_fn(blk)))),
                     jnp.full((SUBLANES, tq), jnp.inf, F32))
            return jnp.min(c, axis=0, keepdims=True)

        def verify(lo):
            thr = fold_min(lambda blk: jnp.where(blk >= lo, blk, jnp.inf))
            return thr, count(lambda blk, k0: blk > thr)

        def w_cond(st):
            lo, hi, c_lo, thr, n_gt, it = st
            return jnp.logical_and(any_lane(n_gt >= kk), it < 64)

        def w_body(st):
            lo, hi, c_lo, thr, n_gt, it = st
            lo, hi, c_lo = bisect(t_more, lo, hi, c_lo)
            thr, n_gt = verify(lo)
            return lo, hi, c_lo, thr, n_gt, it + 1

        thr, n_gt = verify(lo)
        _, _, _, thr, n_gt, _ = lax.while_loop(w_cond, w_body, (lo, hi, c_lo, thr, n_gt, jnp.int32(0)))
        need = kk - n_gt
        r_i = lax.broadcasted_iota(jnp.int32, (tk, tk), 0)
        c_i = lax.broadcasted_iota(jnp.int32, (tk, tk), 1)
        earlier = jnp.where(c_i < r_i, 1.0, 0.0).astype(BF16)

        def body(kb, seen):
            k0 = pl.multiple_of(kb * tk, tk)
            blk = s_scr[pl.ds(k0, tk), :]
            tie = blk == thr
            tie01 = jnp.where(tie, 1.0, 0.0)
            rank = seen + jnp.dot(earlier, tie01.astype(BF16), preferred_element_type=F32)
            sel = jnp.logical_or(blk > thr, jnp.logical_and(tie, rank < need))
            s_scr[pl.ds(k0, tk), :] = jnp.where(sel, 0.0, MASK_DIST)
            return rank[tk - 1:tk, :] + tie01[tk - 1:tk, :]
        lax.fori_loop(0, nkb, body, jnp.zeros((1, tq), F32))

    lax.cond(any_lane(c_lo != kk), penalty_with_ties, penalty_plain)

    log2e = math.log2(math.e)
    qs_scr[...] = (q_ref[...].astype(F32) * (SA_HEAD_DIM ** -0.5 * log2e)).astype(BF16)
    for acc_scr in acc_scrs:
        acc_scr[...] = jnp.zeros(acc_scr.shape, F32)

    tk4 = tk * kpb
    nkb4 = (nkb + kpb - 1) // kpb
    key4_t = lax.broadcasted_iota(jnp.int32, (tk4, tq), 0)
    qry4_t = q0 + lax.broadcasted_iota(jnp.int32, (tk4, tq), 1)

    @pl.when(nkb4 * kpb > nkb)
    def _():
        s_scr[pl.ds(pl.multiple_of(nkb * tk, tk), tk), :] = jnp.full((tk, tq), MASK_DIST, F32)

    def att_body(kb, carry):
        m_all, l_all = carry
        k0 = pl.multiple_of(kb * tk4, tk4)
        dist = jnp.abs(qry4_t - (k0 + key4_t)).astype(F32) + s_scr[pl.ds(k0, tk4), :]
        cols = [slice(h * SA_HEAD_DIM, (h + 1) * SA_HEAD_DIM) for h in range(SA_HEADS)]
        m_rows, l_rows = [], []
        for g0 in range(0, SA_HEADS, hg):
            heads = range(g0, g0 + hg)
            qk = {h: _nt_dot(k_ref[pl.ds(k0, tk4), cols[h]], qs_scr[:, cols[h]]) for h in heads}
            alphas, ps = {}, {}
            for h in heads:
                slope2 = 2.0 ** (-8.0 * (h + 1) / SA_HEADS) * log2e
                logits = qk[h] - slope2 * dist
                m_old = m_all[h:h + 1, :]
                m_new = jnp.maximum(m_old, jnp.max(logits, axis=0, keepdims=True))
                alphas[h] = jnp.exp2(m_old - m_new)
                p = jnp.exp2(logits - m_new)
                l_rows.append(alphas[h] * l_all[h:h + 1, :] + jnp.sum(p, axis=0, keepdims=True))
                m_rows.append(m_new)
                ps[h] = p.astype(BF16)
            pv = {h: jnp.dot(vt_ref[cols[h], pl.ds(k0, tk4)], ps[h], preferred_element_type=F32) for h in heads}
            for h in heads:
                acc_scrs[h][...] = alphas[h] * acc_scrs[h][...] + pv[h]
        return jnp.concatenate(m_rows, axis=0), jnp.concatenate(l_rows, axis=0)

    _, l_all = lax.fori_loop(0, nkb4, att_body,
                             (jnp.full((SA_HEADS, tq), M_INIT, F32), jnp.zeros((SA_HEADS, tq), F32)))

    for h in range(SA_HEADS):
        o_ref[:, h * SA_HEAD_DIM:(h + 1) * SA_HEAD_DIM] = jnp.transpose(acc_scrs[h][...] / l_all[h:h + 1, :])


def _dsa(p16, v_t, bsz, seq, tq, tk):
    n = bsz * seq
    steps = seq // tq
    topk = min(MAX_TOPK, seq // 4)
    kpb = 2
    assert seq % (tk * kpb) == 0 and tq % tk == 0
    once = dict(pipeline_mode=pl.Buffered(1))
    qrow = lambda cb: (lambda b, i: (b * steps + i, cb))
    return pl.pallas_call(
        functools.partial(_dsa_kernel, tq=tq, tk=tk, topk=topk, t0=22, t_more=4, hg=SA_HEADS, kpb=kpb),
        grid=(bsz, steps),
        in_specs=[
            pl.BlockSpec((tq, IDX_WIDTH), qrow(O16_IQ // IDX_WIDTH)),
            pl.BlockSpec((tq, LANES), qrow(O16_IKW // LANES)),
            pl.BlockSpec((tq, SA_WIDTH), qrow(O16_Q // SA_WIDTH)),
            pl.BlockSpec((seq, LANES), lambda b, i: (b, O16_IKW // LANES), **once),
            pl.BlockSpec((seq, SA_WIDTH), lambda b, i: (b, O16_K // SA_WIDTH), **once),
            pl.BlockSpec((SA_WIDTH, seq), lambda b, i: (b, 0), **once),
        ],
        out_specs=pl.BlockSpec((tq, SA_WIDTH), lambda b, i: (b * steps + i, 0)),
        out_shape=jax.ShapeDtypeStruct((n, SA_WIDTH), F32),
        scratch_shapes=[
            pltpu.VMEM((seq, tq), F32),
            pltpu.VMEM((tq, SA_WIDTH), BF16),
        ] + [pltpu.VMEM((SA_HEAD_DIM, tq), F32) for _ in range(SA_HEADS)],
        name="dsa",
        compiler_params=pltpu.CompilerParams(
            dimension_semantics=("parallel", "arbitrary"), vmem_limit_bytes=VMEM_LIMIT),
    )(p16, p16, p16, p16, p16, v_t)


def _layer_norm(x, g, b):
    mu = jnp.mean(x, axis=-1, keepdims=True)
    xc = x - mu
    var = jnp.mean(xc * xc, axis=-1, keepdims=True)
    return xc * lax.rsqrt(var + LN_EPS) * g + b


def _merge_kernel(a_ref, b_ref, ga_ref, gb_ref, x_ref, wa_ref, wb_ref, wo_ref, g1_ref, b1_ref,
                  wrh_ref, wrl_ref, br_ref, h_ref, ids_ref, wts_ref):
    merged = (_sigmoid(ga_ref[...]) * jnp.dot(a_ref[...].astype(BF16), wa_ref[...], preferred_element_type=F32)
              + _sigmoid(gb_ref[...]) * jnp.dot(b_ref[...].astype(BF16), wb_ref[...], preferred_element_type=F32))
    mix = jnp.dot(merged.astype(BF16), wo_ref[...], preferred_element_type=F32)
    h = _layer_norm(DN_ALPHA * x_ref[...] + mix, g1_ref[...], b1_ref[...])
    h_ref[...] = h

    h_hi = h.astype(BF16)
    h_lo = (h - h_hi.astype(F32)).astype(BF16)
    logits = (jnp.dot(h_hi, wrh_ref[...], preferred_element_type=F32)
              + jnp.dot(h_lo, wrh_ref[...], preferred_element_type=F32)
              + jnp.dot(h_hi, wrl_ref[...], preferred_element_type=F32)) + br_ref[...]
    lane = lax.broadcasted_iota(jnp.int32, logits.shape, 1)
    big = jnp.int32(4 * LANES)

    def first_argmax(vals):
        mval = jnp.max(vals, axis=-1, keepdims=True)
        idx = jnp.min(jnp.where(vals == mval, lane, big), axis=-1, keepdims=True)
        return mval, idx

    gl = jnp.where(lane < N_GROUPS, logits, -jnp.inf)
    g_max, g_idx = first_argmax(gl)
    g_top_p = 1.0 / jnp.sum(jnp.exp(gl - g_max), axis=-1, keepdims=True)
    e_lo = N_GROUPS + g_idx * EXPERTS_PER_GROUP
    el = jnp.where(jnp.logical_and(lane >= e_lo, lane < e_lo + EXPERTS_PER_GROUP), logits, -jnp.inf)
    e1, i1 = first_argmax(el)
    e2, i2 = first_argmax(jnp.where(lane == i1, -jnp.inf, el))
    t = jnp.exp(e2 - e1)
    w1 = (1.0 / (1.0 + t)) * g_top_p
    w2 = (t / (1.0 + t)) * g_top_p
    ids_ref[...] = jnp.where(lane == 0, i1 - N_GROUPS, jnp.where(lane == 1, i2 - N_GROUPS, 0))
    wts_ref[...] = jnp.where(lane == 0, w1, jnp.where(lane == 1, w2, 0.0))


def _merge(branch_a, branch_b, p32, x2, wa, wb, wo, g1, b1, wr_hi, wr_lo, br, tm):
    n, d = x2.shape
    once = dict(pipeline_mode=pl.Buffered(1))
    ga_blk = O32_GATES // d
    row = lambda i: (i, 0)
    const = lambda i: (0, 0)
    return pl.pallas_call(
        _merge_kernel,
        grid=(n // tm,),
        in_specs=[
            pl.BlockSpec((tm, HG_WIDTH), row),
            pl.BlockSpec((tm, SA_WIDTH), row),
            pl.BlockSpec((tm, d), lambda i: (i, ga_blk)),
            pl.BlockSpec((tm, d), lambda i: (i, ga_blk + 1)),
            pl.BlockSpec((tm, d), row),
            pl.BlockSpec((HG_WIDTH, d), const, **once),
            pl.BlockSpec((SA_WIDTH, d), const, **once),
            pl.BlockSpec((d, d), const, **once),
            pl.BlockSpec((1, d), const),
            pl.BlockSpec((1, d), const),
            pl.BlockSpec((d, LANES), const, **once),
            pl.BlockSpec((d, LANES), const, **once),
            pl.BlockSpec((1, LANES), const),
        ],
        out_specs=[pl.BlockSpec((tm, d), row), pl.BlockSpec((tm, LANES), row), pl.BlockSpec((tm, LANES), row)],
        out_shape=[jax.ShapeDtypeStruct((n, d), F32),
                   jax.ShapeDtypeStruct((n, LANES), jnp.int32),
                   jax.ShapeDtypeStruct((n, LANES), F32)],
        name="merge_ln1_router",
        compiler_params=pltpu.CompilerParams(
            dimension_semantics=("parallel",), vmem_limit_bytes=VMEM_LIMIT),
    )(branch_a, branch_b, p32, p32, x2, wa, wb, wo, g1, b1, wr_hi, wr_lo, br)


def _row_gather(idx_ref, base, n_rows, src_hbm, dst, sem):
    for r in range(n_rows):
        pltpu.make_async_copy(src_hbm.at[pl.ds(idx_ref[base + r], 1)], dst.at[pl.ds(r, 1)], sem).start()


def _row_gather_wait(n_rows, src_hbm, dst, sem):
    pltpu.make_async_copy(src_hbm.at[pl.ds(0, n_rows)], dst, sem).wait()


def _moe_up_kernel(be_ref, nu_ref, rt_ref, h_hbm, wg_ref, wu_ref, o_ref, xbuf, sem, *, bm, n_blocks):
    i = pl.program_id(0)
    nu = nu_ref[0]
    slot = i % 2

    @pl.when(i == 0)
    def _():
        _row_gather(rt_ref, 0, bm, h_hbm, xbuf.at[0], sem.at[0])

    @pl.when(i < nu)
    def _():
        nxt = jnp.minimum(i + 1, n_blocks - 1)
        _row_gather(rt_ref, nxt * bm, bm, h_hbm, xbuf.at[1 - slot], sem.at[1 - slot])
        _row_gather_wait(bm, h_hbm, xbuf.at[slot], sem.at[slot])
        xb = xbuf[slot].astype(BF16)
        g = jnp.dot(xb, wg_ref[0], preferred_element_type=F32)
        u = jnp.dot(xb, wu_ref[0], preferred_element_type=F32)
        o_ref[...] = (_silu(g) * u).astype(BF16)

        @pl.when(i == n_blocks - 1)
        def _():
            _row_gather_wait(bm, h_hbm, xbuf.at[1 - slot], sem.at[1 - slot])

    @pl.when(i >= nu)
    def _():
        @pl.when(i == nu)
        def _():
            _row_gather_wait(bm, h_hbm, xbuf.at[slot], sem.at[slot])
        o_ref[...] = jnp.zeros_like(o_ref)


def _moe_down_kernel(be_ref, nu_ref, h_ref, wd_ref, o_ref):
    @pl.when(pl.program_id(0) < nu_ref[0])
    def _():
        o_ref[...] = jnp.dot(h_ref[...], wd_ref[0], preferred_element_type=F32)

    @pl.when(pl.program_id(0) >= nu_ref[0])
    def _():
        o_ref[...] = jnp.zeros_like(o_ref)


def _moe_ffn(h1, row_tok, block_e, n_used, wg, wu, wd, bm):
    n_rows = row_tok.shape[0]
    d = h1.shape[1]
    f = wg.shape[2]
    n_blocks = n_rows // bm
    params = pltpu.CompilerParams(dimension_semantics=("arbitrary",), vmem_limit_bytes=VMEM_LIMIT)
    hb = pl.pallas_call(
        functools.partial(_moe_up_kernel, bm=bm, n_blocks=n_blocks),
        grid_spec=pltpu.PrefetchScalarGridSpec(
            num_scalar_prefetch=3, grid=(n_blocks,),
            in_specs=[pl.BlockSpec(memory_space=pl.ANY),
                      pl.BlockSpec((1, d, f), lambda i, be, nu, rt: (be[i], 0, 0)),
                      pl.BlockSpec((1, d, f), lambda i, be, nu, rt: (be[i], 0, 0))],
            out_specs=pl.BlockSpec((bm, f), lambda i, be, nu, rt: (i, 0)),
            scratch_shapes=[pltpu.VMEM((2, bm, d), F32), pltpu.SemaphoreType.DMA((2,))]),
        out_shape=jax.ShapeDtypeStruct((n_rows, f), BF16),
        name="moe_up", compiler_params=params,
    )(block_e, n_used, row_tok, h1, wg, wu)
    return pl.pallas_call(
        _moe_down_kernel,
        grid_spec=pltpu.PrefetchScalarGridSpec(
            num_scalar_prefetch=2, grid=(n_blocks,),
            in_specs=[pl.BlockSpec((bm, f), lambda i, be, nu: (i, 0)),
                      pl.BlockSpec((1, f, d), lambda i, be, nu: (be[i], 0, 0))],
            out_specs=pl.BlockSpec((bm, d), lambda i, be, nu: (i, 0))),
        out_shape=jax.ShapeDtypeStruct((n_rows, d), F32),
        name="moe_down", compiler_params=params,
    )(block_e, n_used, hb, wd)


def _combine_kernel(dest_ref, ys_hbm, h_ref, wts_ref, g2_ref, b2_ref, o_ref, ybuf, sem, *, tm, n_steps):
    i = pl.program_id(0)
    slot = i % 2

    @pl.when(i == 0)
    def _():
        _row_gather(dest_ref, 0, 2 * tm, ys_hbm, ybuf.at[0], sem.at[0])

    nxt = jnp.minimum(i + 1, n_steps - 1)
    _row_gather(dest_ref, nxt * 2 * tm, 2 * tm, ys_hbm, ybuf.at[1 - slot], sem.at[1 - slot])
    _row_gather_wait(2 * tm, ys_hbm, ybuf.at[slot], sem.at[slot])
    w0 = wts_ref[:, 0:1]
    w1 = wts_ref[:, 1:2]
    ffn = w0 * ybuf[slot, 0:tm, :] + w1 * ybuf[slot, tm:2 * tm, :]
    o_ref[...] = _layer_norm(DN_ALPHA * h_ref[...] + ffn, g2_ref[...], b2_ref[...])

    @pl.when(i == n_steps - 1)
    def _():
        _row_gather_wait(2 * tm, ys_hbm, ybuf.at[1 - slot], sem.at[1 - slot])


def _combine(h1, ys, dest_tiles, wts, g2, b2, tm):
    n, d = h1.shape
    n_steps = n // tm
    row = lambda i, dst: (i, 0)
    const = lambda i, dst: (0, 0)
    return pl.pallas_call(
        functools.partial(_combine_kernel, tm=tm, n_steps=n_steps),
        grid_spec=pltpu.PrefetchScalarGridSpec(
            num_scalar_prefetch=1, grid=(n_steps,),
            in_specs=[pl.BlockSpec(memory_space=pl.ANY), pl.BlockSpec((tm, d), row),
                      pl.BlockSpec((tm, LANES), row), pl.BlockSpec((1, d), const), pl.BlockSpec((1, d), const)],
            out_specs=pl.BlockSpec((tm, d), row),
            scratch_shapes=[pltpu.VMEM((2, 2 * tm, d), F32), pltpu.SemaphoreType.DMA((2,))]),
        out_shape=jax.ShapeDtypeStruct((n, d), F32),
        name="combine_ln2",
        compiler_params=pltpu.CompilerParams(
            dimension_semantics=("arbitrary",), vmem_limit_bytes=VMEM_LIMIT),
    )(dest_tiles, ys, h1, wts, g2, b2)


def _dispatch_plan(e_ids, bm):
    n_tok = e_ids.shape[0]
    n_assign = n_tok * TOPK_IN_GROUP
    e = e_ids.reshape(-1)
    onehot = (e[:, None] == jnp.arange(N_EXPERTS, dtype=jnp.int32)[None, :]).astype(jnp.int32)
    cum = jnp.cumsum(onehot, axis=0)
    rank = jnp.take_along_axis(cum, e[:, None], axis=1)[:, 0] - 1
    counts = cum[-1]
    padded = (counts + bm - 1) // bm * bm
    pend = jnp.cumsum(padded)
    pstart = pend - padded
    dest = pstart[e] + rank
    n_blocks = -(-(n_assign + N_EXPERTS * (bm - 1)) // bm)
    n_rows = n_blocks * bm
    tok = jnp.repeat(jnp.arange(n_tok, dtype=jnp.int32), TOPK_IN_GROUP)
    row_tok = jnp.zeros((n_rows,), jnp.int32).at[dest].set(tok)
    block_pos = jnp.arange(n_blocks, dtype=jnp.int32) * bm
    block_e = jnp.sum((pend[None, :] <= block_pos[:, None]).astype(jnp.int32), axis=1)
    block_e = jnp.minimum(block_e, N_EXPERTS - 1)
    n_used = (pend[-1] // bm).astype(jnp.int32).reshape(1)
    return row_tok, dest.reshape(n_tok, TOPK_IN_GROUP), block_e, n_used


def kernel(x, w_in, b_in, hg_lb_logits, hg_norm_g, w_branch_a, w_branch_b, w_out, ln1_g, ln1_b,
           w_group, b_group, w_router, b_router, w_gate, w_up, w_down, ln2_g, ln2_b):
    bsz, seq, d = x.shape
    n = bsz * seq
    x2 = x.reshape(n, d)
    l = 0

    def packed(src):
        kw = src[..., MAIN_COLS:MAIN_COLS + IDX_DIM + IDX_HEADS]
        pad = jnp.zeros(src.shape[:-1] + (IKW_COLS - IDX_DIM - IDX_HEADS,), src.dtype)
        return jnp.concatenate([src[..., :MAIN_COLS], kw, pad, src[..., MAIN_COLS + IDX_DIM + IDX_HEADS:]], axis=-1)

    w_all = packed(w_in[l]).astype(BF16)
    tn = 1024 if w_all.shape[1] % 1024 == 0 else 512
    p32, p16 = _in_proj(x2, w_all, packed(b_in[l]).reshape(1, -1), tm=min(1024, n), tn=tn)

    branch_a = _hgrn(p32, hg_lb_logits, hg_norm_g[l].reshape(1, HG_WIDTH), bsz, seq, min(512, seq))

    v_t = p16[:, O16_V:O16_V + SA_WIDTH].reshape(bsz, seq, SA_WIDTH).transpose(0, 2, 1).reshape(bsz * SA_WIDTH, seq)
    tq = min(256, seq)
    branch_b = _dsa(p16, v_t, bsz, seq, tq=tq, tk=min(256, tq))

    wr = jnp.zeros((d, LANES), F32)
    wr = wr.at[:, :N_GROUPS].set(w_group[l]).at[:, N_GROUPS:N_GROUPS + N_EXPERTS].set(w_router[l])
    wr_hi = wr.astype(BF16)
    wr_lo = (wr - wr_hi.astype(F32)).astype(BF16)
    br = jnp.zeros((1, LANES), F32)
    br = br.at[0, :N_GROUPS].set(b_group[l]).at[0, N_GROUPS:N_GROUPS + N_EXPERTS].set(b_router[l])
    tm = min(256, n)
    h1, ids, wts = _merge(branch_a, branch_b, p32, x2,
                          w_branch_a[l].astype(BF16), w_branch_b[l].astype(BF16), w_out[l].astype(BF16),
                          ln1_g[l].reshape(1, d), ln1_b[l].reshape(1, d), wr_hi, wr_lo, br, tm)

    bm = 256
    row_tok, dest, block_e, n_used = _dispatch_plan(ids[:, :TOPK_IN_GROUP], bm)
    ys = _moe_ffn(h1, row_tok, block_e, n_used, w_gate[l], w_up[l], w_down[l], bm)
    dest_tiles = dest.reshape(n // tm, tm, TOPK_IN_GROUP).transpose(0, 2, 1).reshape(-1)
    out = _combine(h1, ys, dest_tiles, wts, ln2_g[l].reshape(1, d), ln2_b[l].reshape(1, d), tm)
    return out.reshape(bsz, seq, d)
```

```python
import functools
import math

import jax
import jax.numpy as jnp
from jax import lax
from jax.experimental import pallas as pl
from jax.experimental.pallas import tpu as pltpu

CHUNK = 64
HG_HEADS = 8
HG_DK = 128
HG_DV = 128
HG_WIDTH = HG_HEADS * HG_DK
SA_HEADS = 8
SA_HEAD_DIM = 128
SA_WIDTH = SA_HEADS * SA_HEAD_DIM
IDX_HEADS = 8
IDX_DIM = 64
IDX_WIDTH = IDX_HEADS * IDX_DIM
MAX_TOPK = 256
N_GROUPS = 4
EXPERTS_PER_GROUP = 8
N_EXPERTS = N_GROUPS * EXPERTS_PER_GROUP
TOPK_IN_GROUP = 2
DEPTH = 1
DN_ALPHA = (2.0 * DEPTH) ** 0.25
LN_EPS = 1e-5
RMS_EPS = 1e-6

LANES = 128
SUBLANES = 8
VMEM_LIMIT = 56 * 1024 * 1024
MASK_DIST = 1e30
M_INIT = -1e30

F32 = jnp.float32
BF16 = jnp.bfloat16

MAIN_COLS = 4 * HG_WIDTH + 3 * SA_WIDTH + IDX_WIDTH
IKW_COLS = 512
O16_LO = 4 * HG_WIDTH
O16_Q, O16_K, O16_V, O16_IQ, O16_IKW = 0, SA_WIDTH, 2 * SA_WIDTH, 3 * SA_WIDTH, 3 * SA_WIDTH + IDX_WIDTH
O16_GATES = 3 * SA_WIDTH + IDX_WIDTH + IKW_COLS


def _nt_dot(a, b):
    return lax.dot_general(a, b, (((1,), (1,)), ((), ())), preferred_element_type=F32)


def _tn_dot(a, b):
    return lax.dot_general(a, b, (((0,), (0,)), ((), ())), preferred_element_type=F32)


def _sigmoid(x):
    return 1.0 / (1.0 + jnp.exp(-x))


def _silu(x):
    return x * _sigmoid(x)


def _proj_kernel(x_ref, w_ref, b_ref, o32_ref, o16_ref, xb_ref, *, j16_lo):
    j = pl.program_id(1)

    @pl.when(j == 0)
    def _():
        xb_ref[...] = x_ref[...].astype(BF16)

    def block():
        return jnp.dot(xb_ref[...], w_ref[...], preferred_element_type=F32) + b_ref[...]

    @pl.when(j < j16_lo)
    def _():
        o32_ref[...] = block()

    @pl.when(j >= j16_lo)
    def _():
        o16_ref[...] = block().astype(BF16)


def _in_proj(x2, w_all, b_all, tm, tn):
    n, d = x2.shape
    cols = w_all.shape[1]
    lo = O16_LO // tn
    return pl.pallas_call(
        functools.partial(_proj_kernel, j16_lo=lo),
        grid=(n // tm, cols // tn),
        in_specs=[
            pl.BlockSpec((tm, d), lambda i, j: (i, 0)),
            pl.BlockSpec((d, tn), lambda i, j: (0, j)),
            pl.BlockSpec((1, tn), lambda i, j: (0, j)),
        ],
        out_specs=[
            pl.BlockSpec((tm, tn), lambda i, j: (i, jnp.minimum(j, lo - 1))),
            pl.BlockSpec((tm, tn), lambda i, j: (i, jnp.maximum(j - lo, 0))),
        ],
        out_shape=[jax.ShapeDtypeStruct((n, O16_LO), F32), jax.ShapeDtypeStruct((n, cols - O16_LO), BF16)],
        scratch_shapes=[pltpu.VMEM((tm, d), BF16)],
        name="in_proj",
        compiler_params=pltpu.CompilerParams(
            dimension_semantics=("parallel", "arbitrary"), vmem_limit_bytes=VMEM_LIMIT),
    )(x2, w_all, b_all)


def _hgrn_kernel(q_ref, f_ref, i_ref, g_ref, lbl_ref, ng_ref, o_ref, st_ref, *, n_chunks):
    @pl.when(pl.program_id(1) == 0)
    def _():
        st_ref[...] = jnp.zeros_like(st_ref)

    lbl = lbl_ref[...].astype(F32)
    lbe = jnp.exp(lbl - jnp.max(lbl, axis=0, keepdims=True))
    lb = lbe[0:1, :] / jnp.sum(lbe, axis=0, keepdims=True)

    r_i = lax.broadcasted_iota(jnp.int32, (CHUNK, CHUNK), 0)
    c_i = lax.broadcasted_iota(jnp.int32, (CHUNK, CHUNK), 1)
    causal = c_i <= r_i
    tri = jnp.where(causal, 1.0, 0.0).astype(BF16)

    def chunk_body(c, carry):
        r0 = pl.multiple_of(c * CHUNK, CHUNK)
        rows = pl.ds(r0, CHUNK)
        heads = range(HG_HEADS)
        cols = [slice(h * HG_DK, (h + 1) * HG_DK) for h in heads]
        f = [lb[:, cols[h]] + (1.0 - lb[:, cols[h]]) * _sigmoid(f_ref[rows, cols[h]]) for h in heads]
        logf = [jnp.log(f[h]) for h in heads]
        hi = [logf[h].astype(BF16) for h in heads]
        lo = [(logf[h] - hi[h].astype(F32)).astype(BF16) for h in heads]
        cum = [jnp.dot(tri, hi[h], preferred_element_type=F32) + jnp.dot(tri, lo[h], preferred_element_type=F32)
               for h in heads]
        cum_last = [cum[h][CHUNK - 1:CHUNK, :] for h in heads]
        vh = [i_ref[rows, cols[h]].astype(BF16) for h in heads]
        q_dec = [(_silu(q_ref[rows, cols[h]]) * jnp.exp(cum[h])).astype(BF16) for h in heads]
        k_dec = [((1.0 - f[h]) * jnp.exp(-cum[h])).astype(BF16) for h in heads]
        k_end = [((1.0 - f[h]) * jnp.exp(cum_last[h] - cum[h])).astype(BF16) for h in heads]
        st = [st_ref[h] for h in heads]
        att = [jnp.where(causal, _nt_dot(q_dec[h], k_dec[h]), 0.0).astype(BF16) for h in heads]
        o_st = [_nt_dot(q_dec[h], st[h].astype(BF16)) for h in heads]
        kv = [_tn_dot(vh[h], k_end[h]) for h in heads]
        o = [jnp.dot(att[h], vh[h], preferred_element_type=F32) + o_st[h] for h in heads]
        for h in heads:
            st_ref[h] = st[h] * jnp.exp(cum_last[h]) + kv[h]
            on = o[h] * lax.rsqrt(jnp.mean(o[h] * o[h], axis=-1, keepdims=True) + RMS_EPS)
            o_ref[rows, cols[h]] = (on * ng_ref[:, cols[h]] * _silu(g_ref[rows, cols[h]])).astype(BF16)
        return carry

    lax.fori_loop(0, n_chunks, chunk_body, 0)


def _hgrn(p32, lb_logits, norm_g, bsz, seq, rows_per_step):
    n = bsz * seq
    w = HG_WIDTH
    steps = seq // rows_per_step
    nl = lb_logits.shape[0]

    def col_spec(cb):
        return pl.BlockSpec((rows_per_step, w), lambda b, s: (b * steps + s, cb))

    return pl.pallas_call(
        functools.partial(_hgrn_kernel, n_chunks=rows_per_step // CHUNK),
        grid=(bsz, steps),
        in_specs=[col_spec(0), col_spec(1), col_spec(2), col_spec(3),
                  pl.BlockSpec((nl, w), lambda b, s: (0, 0)),
                  pl.BlockSpec((1, w), lambda b, s: (0, 0))],
        out_specs=pl.BlockSpec((rows_per_step, w), lambda b, s: (b * steps + s, 0)),
        out_shape=jax.ShapeDtypeStruct((n, w), BF16),
        scratch_shapes=[pltpu.VMEM((HG_HEADS, HG_DV, HG_DK), F32)],
        name="hgrn",
        compiler_params=pltpu.CompilerParams(
            dimension_semantics=("parallel", "arbitrary"), vmem_limit_bytes=VMEM_LIMIT),
    )(p32, p32, p32, p32, lb_logits, norm_g)


def _dsa_kernel(qi_ref, qw_ref, q_ref, kiw_ref, k_ref, vt_ref, o_ref,
                s_scr, qs_scr, *acc_scrs, tq, tk, topk, t0, t_more, hg, kpb):
    i = pl.program_id(1)
    nkb = (i + 1) * (tq // tk)
    q0 = i * tq
    nsub = tk // SUBLANES

    qcol = q0 + lax.broadcasted_iota(jnp.int32, (1, tq), 1)
    kk = jnp.minimum((qcol // CHUNK + 1) * CHUNK, topk).astype(F32)

    w_t = (jnp.transpose(qw_ref[...].astype(F32))[IDX_DIM:IDX_DIM + IDX_HEADS, :]
           * (IDX_HEADS ** -0.5 * IDX_DIM ** -0.5))

    def tree(op, x):
        while x.shape[0] > 1:
            half = x.shape[0] // 2
            x = op(x[:half], x[half:])
        return x[0]

    def by_vreg(x):
        return x.reshape(nsub, SUBLANES, tq)

    tk1 = tk * kpb
    key1_t = lax.broadcasted_iota(jnp.int32, (tk1, tq), 0)
    qchunk1_t = (q0 + lax.broadcasted_iota(jnp.int32, (tk1, tq), 1)) // CHUNK

    def score_body(kb, carry):
        mn8, mx8 = carry
        k0 = pl.multiple_of(kb * tk1, tk1)
        kblk = kiw_ref[pl.ds(k0, tk1), :][:, :IDX_DIM]
        ds = [_nt_dot(kblk, qi_ref[:, h * IDX_DIM:(h + 1) * IDX_DIM]) for h in range(IDX_HEADS)]
        acc = w_t[0:1, :] * jnp.maximum(ds[0], 0.0)
        for h in range(1, IDX_HEADS):
            acc = acc + w_t[h:h + 1, :] * jnp.maximum(ds[h], 0.0)
        adm = (k0 + key1_t) // CHUNK <= qchunk1_t
        s_scr[pl.ds(k0, tk1), :] = jnp.where(adm, acc, -jnp.inf)
        by8 = lambda x: x.reshape(tk1 // SUBLANES, SUBLANES, tq)
        mn8 = jnp.minimum(mn8, tree(jnp.minimum, by8(jnp.where(adm, acc, jnp.inf))))
        mx8 = jnp.maximum(mx8, tree(jnp.maximum, by8(jnp.where(adm, acc, -jnp.inf))))
        return mn8, mx8

    mn8, mx8 = lax.fori_loop(0, (nkb + kpb - 1) // kpb, score_body,
                             (jnp.full((SUBLANES, tq), jnp.inf, F32), jnp.full((SUBLANES, tq), -jnp.inf, F32)))
    mn = jnp.min(mn8, axis=0, keepdims=True)
    mx = jnp.max(mx8, axis=0, keepdims=True)

    def fold(fn, init):
        def body(kb, c):
            k0 = pl.multiple_of(kb * tk, tk)
            return fn(c, s_scr[pl.ds(k0, tk), :], k0)
        return lax.fori_loop(0, nkb, body, init)

    def count(pred_fn):
        c = fold(lambda c, blk, k0: c + tree(jnp.add, by_vreg(jnp.where(pred_fn(blk, k0), 1.0, 0.0))),
                 jnp.zeros((SUBLANES, tq), F32))
        return jnp.sum(c, axis=0, keepdims=True)

    def any_lane(pred):
        return jnp.max(jnp.where(pred, 1.0, 0.0)) > 0.0

    def bisect(n_it, lo, hi, c_lo):
        def it(_, st):
            lo, hi, c_lo = st
            mid = 0.5 * lo + 0.5 * hi
            c = count(lambda blk, k0: blk >= mid)
            ok = c >= kk
            return jnp.where(ok, mid, lo), jnp.where(ok, hi, mid), jnp.where(ok, c, c_lo)
        return lax.fori_loop(0, n_it, it, (lo, hi, c_lo))

    def b_cond(st):
        lo, hi, c_lo, it = st
        return jnp.logical_and(any_lane(c_lo != kk), it < t0)

    def b_body(st):
        lo, hi, c_lo, it = st
        lo, hi, c_lo = bisect(2, lo, hi, c_lo)
        return lo, hi, c_lo, it + 2

    n_adm = ((qcol // CHUNK + 1) * CHUNK).astype(F32)
    lo, hi, c_lo, _ = lax.while_loop(b_cond, b_body, (mn, mx, n_adm, jnp.int32(0)))

    def penalty_plain():
        def body(kb, carry):
            k0 = pl.multiple_of(kb * tk, tk)
            s_scr[pl.ds(k0, tk), :] = jnp.where(s_scr[pl.ds(k0, tk), :] >= lo, 0.0, MASK_DIST)
            return carry
        lax.fori_loop(0, nkb, body, 0)

    def penalty_with_ties():
        def fold_min(sel_fn):
            c = fold(lambda c, blk, k0: jnp.minimum(c, tree(jnp.minimum, by_vreg(sel_fn(blk)))),
                     jnp.full((SUBLANES, tq), jnp.inf, F32))
            return jnp.min(c, axis=0, keepdims=True)

        def verify(lo):
            thr = fold_min(lambda blk: jnp.where(blk >= lo, blk, jnp.inf))
            return thr, count(lambda blk, k0: blk > thr)

        def w_cond(st):
            lo, hi, c_lo, thr, n_gt, it = st
            return jnp.logical_and(any_lane(n_gt >= kk), it < 64)

        def w_body(st):
            lo, hi, c_lo, thr, n_gt, it = st
            lo, hi, c_lo = bisect(t_more, lo, hi, c_lo)
            thr, n_gt = verify(lo)
            return lo, hi, c_lo, thr, n_gt, it + 1

        thr, n_gt = verify(lo)
        _, _, _, thr, n_gt, _ = lax.while_loop(w_cond, w_body, (lo, hi, c_lo, thr, n_gt, jnp.int32(0)))
        need = kk - n_gt
        r_i = lax.broadcasted_iota(jnp.int32, (tk, tk), 0)
        c_i = lax.broadcasted_iota(jnp.int32, (tk, tk), 1)
        earlier = jnp.where(c_i < r_i, 1.0, 0.0).astype(BF16)

        def body(kb, seen):
            k0 = pl.multiple_of(kb * tk, tk)
            blk = s_scr[pl.ds(k0, tk), :]
            tie = blk == thr
            tie01 = jnp.where(tie, 1.0, 0.0)
            rank = seen + jnp.dot(earlier, tie01.astype(BF16), preferred_element_type=F32)
            sel = jnp.logical_or(blk > thr, jnp.logical_and(tie, rank < need))
            s_scr[pl.ds(k0, tk), :] = jnp.where(sel, 0.0, MASK_DIST)
            return rank[tk - 1:tk, :] + tie01[tk - 1:tk, :]
        lax.fori_loop(0, nkb, body, jnp.zeros((1, tq), F32))

    lax.cond(any_lane(c_lo != kk), penalty_with_ties, penalty_plain)

    log2e = math.log2(math.e)
    qs_scr[...] = (q_ref[...].astype(F32) * (SA_HEAD_DIM ** -0.5 * log2e)).astype(BF16)
    for acc_scr in acc_scrs:
        acc_scr[...] = jnp.zeros(acc_scr.shape, F32)

    tk4 = tk * kpb
    nkb4 = (nkb + kpb - 1) // kpb
    key4_t = lax.broadcasted_iota(jnp.int32, (tk4, tq), 0)
    qry4_t = q0 + lax.broadcasted_iota(jnp.int32, (tk4, tq), 1)

    @pl.when(nkb4 * kpb > nkb)
    def _():
        s_scr[pl.ds(pl.multiple_of(nkb * tk, tk), tk), :] = jnp.full((tk, tq), MASK_DIST, F32)

    def att_body(kb, carry):
        m_all, l_all = carry
        k0 = pl.multiple_of(kb * tk4, tk4)
        dist = jnp.abs(qry4_t - (k0 + key4_t)).astype(F32) + s_scr[pl.ds(k0, tk4), :]
        cols = [slice(h * SA_HEAD_DIM, (h + 1) * SA_HEAD_DIM) for h in range(SA_HEADS)]
        m_rows, l_rows = [], []
        for g0 in range(0, SA_HEADS, hg):
            heads = range(g0, g0 + hg)
            qk = {h: _nt_dot(k_ref[pl.ds(k0, tk4), cols[h]], qs_scr[:, cols[h]]) for h in heads}
            alphas, ps = {}, {}
            for h in heads:
                slope2 = 2.0 ** (-8.0 * (h + 1) / SA_HEADS) * log2e
                logits = qk[h] - slope2 * dist
                m_old = m_all[h:h + 1, :]
                m_new = jnp.maximum(m_old, jnp.max(logits, axis=0, keepdims=True))
                alphas[h] = jnp.exp2(m_old - m_new)
                p = jnp.exp2(logits - m_new)
                l_rows.append(alphas[h] * l_all[h:h + 1, :] + jnp.sum(p, axis=0, keepdims=True))
                m_rows.append(m_new)
                ps[h] = p.astype(BF16)
            pv = {h: jnp.dot(vt_ref[cols[h], pl.ds(k0, tk4)], ps[h], preferred_element_type=F32) for h in heads}
            for h in heads:
                acc_scrs[h][...] = alphas[h] * acc_scrs[h][...] + pv[h]
        return jnp.concatenate(m_rows, axis=0), jnp.concatenate(l_rows, axis=0)

    _, l_all = lax.fori_loop(0, nkb4, att_body,
                             (jnp.full((SA_HEADS, tq), M_INIT, F32), jnp.zeros((SA_HEADS, tq), F32)))

    for h in range(SA_HEADS):
        o_ref[:, h * SA_HEAD_DIM:(h + 1) * SA_HEAD_DIM] = jnp.transpose(acc_scrs[h][...] / l_all[h:h + 1, :]).astype(BF16)


def _dsa(p16, v_t, bsz, seq, tq, tk):
    n = bsz * seq
    steps = seq // tq
    topk = min(MAX_TOPK, seq // 4)
    kpb = 2
    assert seq % (tk * kpb) == 0 and tq % tk == 0
    once = dict(pipeline_mode=pl.Buffered(1))
    qrow = lambda cb: (lambda b, i: (b * steps + i, cb))
    return pl.pallas_call(
        functools.partial(_dsa_kernel, tq=tq, tk=tk, topk=topk, t0=22, t_more=4, hg=SA_HEADS, kpb=kpb),
        grid=(bsz, steps),
        in_specs=[
            pl.BlockSpec((tq, IDX_WIDTH), qrow(O16_IQ // IDX_WIDTH)),
            pl.BlockSpec((tq, LANES), qrow(O16_IKW // LANES)),
            pl.BlockSpec((tq, SA_WIDTH), qrow(O16_Q // SA_WIDTH)),
            pl.BlockSpec((seq, LANES), lambda b, i: (b, O16_IKW // LANES), **once),
            pl.BlockSpec((seq, SA_WIDTH), lambda b, i: (b, O16_K // SA_WIDTH), **once),
            pl.BlockSpec((SA_WIDTH, seq), lambda b, i: (b, 0), **once),
        ],
        out_specs=pl.BlockSpec((tq, SA_WIDTH), lambda b, i: (b * steps + i, 0)),
        out_shape=jax.ShapeDtypeStruct((n, SA_WIDTH), BF16),
        scratch_shapes=[
            pltpu.VMEM((seq, tq), F32),
            pltpu.VMEM((tq, SA_WIDTH), BF16),
        ] + [pltpu.VMEM((SA_HEAD_DIM, tq), F32) for _ in range(SA_HEADS)],
        name="dsa",
        compiler_params=pltpu.CompilerParams(
            dimension_semantics=("parallel", "arbitrary"), vmem_limit_bytes=VMEM_LIMIT),
    )(p16, p16, p16, p16, p16, v_t)


def _layer_norm(x, g, b):
    mu = jnp.mean(x, axis=-1, keepdims=True)
    xc = x - mu
    var = jnp.mean(xc * xc, axis=-1, keepdims=True)
    return xc * lax.rsqrt(var + LN_EPS) * g + b


def _merge_kernel(a_ref, b_ref, ga_ref, gb_ref, x_ref, wa_ref, wb_ref, wo_ref, g1_ref, b1_ref,
                  wrh_ref, wrl_ref, br_ref, h_ref, ids_ref, wts_ref):
    merged = (_sigmoid(ga_ref[...].astype(F32)) * jnp.dot(a_ref[...], wa_ref[...], preferred_element_type=F32)
              + _sigmoid(gb_ref[...].astype(F32)) * jnp.dot(b_ref[...], wb_ref[...], preferred_element_type=F32))
    mix = jnp.dot(merged.astype(BF16), wo_ref[...], preferred_element_type=F32)
    h = _layer_norm(DN_ALPHA * x_ref[...] + mix, g1_ref[...], b1_ref[...])
    h_ref[...] = h

    h_hi = h.astype(BF16)
    h_lo = (h - h_hi.astype(F32)).astype(BF16)
    logits = (jnp.dot(h_hi, wrh_ref[...], preferred_element_type=F32)
              + jnp.dot(h_lo, wrh_ref[...], preferred_element_type=F32)
              + jnp.dot(h_hi, wrl_ref[...], preferred_element_type=F32)) + br_ref[...]
    lane = lax.broadcasted_iota(jnp.int32, logits.shape, 1)
    big = jnp.int32(4 * LANES)

    def first_argmax(vals):
        mval = jnp.max(vals, axis=-1, keepdims=True)
        idx = jnp.min(jnp.where(vals == mval, lane, big), axis=-1, keepdims=True)
        return mval, idx

    gl = jnp.where(lane < N_GROUPS, logits, -jnp.inf)
    g_max, g_idx = first_argmax(gl)
    g_top_p = 1.0 / jnp.sum(jnp.exp(gl - g_max), axis=-1, keepdims=True)
    e_lo = N_GROUPS + g_idx * EXPERTS_PER_GROUP
    el = jnp.where(jnp.logical_and(lane >= e_lo, lane < e_lo + EXPERTS_PER_GROUP), logits, -jnp.inf)
    e1, i1 = first_argmax(el)
    e2, i2 = first_argmax(jnp.where(lane == i1, -jnp.inf, el))
    t = jnp.exp(e2 - e1)
    w1 = (1.0 / (1.0 + t)) * g_top_p
    w2 = (t / (1.0 + t)) * g_top_p
    ids_ref[...] = jnp.where(lane == 0, i1 - N_GROUPS, jnp.where(lane == 1, i2 - N_GROUPS, 0))
    wts_ref[...] = jnp.where(lane == 0, w1, jnp.where(lane == 1, w2, 0.0))


def _merge(branch_a, branch_b, p16, x2, wa, wb, wo, g1, b1, wr_hi, wr_lo, br, tm):
    n, d = x2.shape
    once = dict(pipeline_mode=pl.Buffered(1))
    ga_blk = O16_GATES // d
    row = lambda i: (i, 0)
    const = lambda i: (0, 0)
    return pl.pallas_call(
        _merge_kernel,
        grid=(n // tm,),
        in_specs=[
            pl.BlockSpec((tm, HG_WIDTH), row),
            pl.BlockSpec((tm, SA_WIDTH), row),
            pl.BlockSpec((tm, d), lambda i: (i, ga_blk)),
            pl.BlockSpec((tm, d), lambda i: (i, ga_blk + 1)),
            pl.BlockSpec((tm, d), row),
            pl.BlockSpec((HG_WIDTH, d), const, **once),
            pl.BlockSpec((SA_WIDTH, d), const, **once),
            pl.BlockSpec((d, d), const, **once),
            pl.BlockSpec((1, d), const),
            pl.BlockSpec((1, d), const),
            pl.BlockSpec((d, LANES), const, **once),
            pl.BlockSpec((d, LANES), const, **once),
            pl.BlockSpec((1, LANES), const),
        ],
        out_specs=[pl.BlockSpec((tm, d), row), pl.BlockSpec((tm, LANES), row), pl.BlockSpec((tm, LANES), row)],
        out_shape=[jax.ShapeDtypeStruct((n, d), F32),
                   jax.ShapeDtypeStruct((n, LANES), jnp.int32),
                   jax.ShapeDtypeStruct((n, LANES), F32)],
        name="merge_ln1_router",
        compiler_params=pltpu.CompilerParams(
            dimension_semantics=("parallel",), vmem_limit_bytes=VMEM_LIMIT),
    )(branch_a, branch_b, p16, p16, x2, wa, wb, wo, g1, b1, wr_hi, wr_lo, br)


def _row_gather(idx_ref, base, n_rows, src_hbm, dst, sem):
    for r in range(n_rows):
        pltpu.make_async_copy(src_hbm.at[pl.ds(idx_ref[base + r], 1)], dst.at[pl.ds(r, 1)], sem).start()


def _row_gather_wait(n_rows, src_hbm, dst, sem):
    pltpu.make_async_copy(src_hbm.at[pl.ds(0, n_rows)], dst, sem).wait()


def _moe_up_kernel(be_ref, nu_ref, rt_ref, h_hbm, wg_ref, wu_ref, o_ref, xbuf, sem, *, bm, n_blocks):
    i = pl.program_id(0)
    nu = nu_ref[0]
    slot = i % 2

    @pl.when(i == 0)
    def _():
        _row_gather(rt_ref, 0, bm, h_hbm, xbuf.at[0], sem.at[0])

    @pl.when(i < nu)
    def _():
        nxt = jnp.minimum(i + 1, n_blocks - 1)
        _row_gather(rt_ref, nxt * bm, bm, h_hbm, xbuf.at[1 - slot], sem.at[1 - slot])
        _row_gather_wait(bm, h_hbm, xbuf.at[slot], sem.at[slot])
        xb = xbuf[slot].astype(BF16)
        g = jnp.dot(xb, wg_ref[0], preferred_element_type=F32)
        u = jnp.dot(xb, wu_ref[0], preferred_element_type=F32)
        o_ref[...] = (_silu(g) * u).astype(BF16)

        @pl.when(i == n_blocks - 1)
        def _():
            _row_gather_wait(bm, h_hbm, xbuf.at[1 - slot], sem.at[1 - slot])

    @pl.when(i >= nu)
    def _():
        @pl.when(i == nu)
        def _():
            _row_gather_wait(bm, h_hbm, xbuf.at[slot], sem.at[slot])
        o_ref[...] = jnp.zeros_like(o_ref)


def _moe_down_kernel(be_ref, nu_ref, h_ref, wd_ref, o_ref):
    @pl.when(pl.program_id(0) < nu_ref[0])
    def _():
        o_ref[...] = jnp.dot(h_ref[...], wd_ref[0], preferred_element_type=F32)

    @pl.when(pl.program_id(0) >= nu_ref[0])
    def _():
        o_ref[...] = jnp.zeros_like(o_ref)


def _moe_ffn(h1, row_tok, block_e, n_used, wg, wu, wd, bm):
    n_rows = row_tok.shape[0]
    d = h1.shape[1]
    f = wg.shape[2]
    n_blocks = n_rows // bm
    params = pltpu.CompilerParams(dimension_semantics=("arbitrary",), vmem_limit_bytes=VMEM_LIMIT)
    hb = pl.pallas_call(
        functools.partial(_moe_up_kernel, bm=bm, n_blocks=n_blocks),
        grid_spec=pltpu.PrefetchScalarGridSpec(
            num_scalar_prefetch=3, grid=(n_blocks,),
            in_specs=[pl.BlockSpec(memory_space=pl.ANY),
                      pl.BlockSpec((1, d, f), lambda i, be, nu, rt: (be[i], 0, 0)),
                      pl.BlockSpec((1, d, f), lambda i, be, nu, rt: (be[i], 0, 0))],
            out_specs=pl.BlockSpec((bm, f), lambda i, be, nu, rt: (i, 0)),
            scratch_shapes=[pltpu.VMEM((2, bm, d), F32), pltpu.SemaphoreType.DMA((2,))]),
        out_shape=jax.ShapeDtypeStruct((n_rows, f), BF16),
        name="moe_up", compiler_params=params,
    )(block_e, n_used, row_tok, h1, wg, wu)
    return pl.pallas_call(
        _moe_down_kernel,
        grid_spec=pltpu.PrefetchScalarGridSpec(
            num_scalar_prefetch=2, grid=(n_blocks,),
            in_specs=[pl.BlockSpec((bm, f), lambda i, be, nu: (i, 0)),
                      pl.BlockSpec((1, f, d), lambda i, be, nu: (be[i], 0, 0))],
            out_specs=pl.BlockSpec((bm, d), lambda i, be, nu: (i, 0))),
        out_shape=jax.ShapeDtypeStruct((n_rows, d), F32),
        name="moe_down", compiler_params=params,
    )(block_e, n_used, hb, wd)


def _combine_kernel(dest_ref, ys_hbm, h_ref, wts_ref, g2_ref, b2_ref, o_ref, ybuf, sem, *, tm, n_steps):
    i = pl.program_id(0)
    slot = i % 2

    @pl.when(i == 0)
    def _():
        _row_gather(dest_ref, 0, 2 * tm, ys_hbm, ybuf.at[0], sem.at[0])

    nxt = jnp.minimum(i + 1, n_steps - 1)
    _row_gather(dest_ref, nxt * 2 * tm, 2 * tm, ys_hbm, ybuf.at[1 - slot], sem.at[1 - slot])
    _row_gather_wait(2 * tm, ys_hbm, ybuf.at[slot], sem.at[slot])
    w0 = wts_ref[:, 0:1]
    w1 = wts_ref[:, 1:2]
    ffn = w0 * ybuf[slot, 0:tm, :] + w1 * ybuf[slot, tm:2 * tm, :]
    o_ref[...] = _layer_norm(DN_ALPHA * h_ref[...] + ffn, g2_ref[...], b2_ref[...])

    @pl.when(i == n_steps - 1)
    def _():
        _row_gather_wait(2 * tm, ys_hbm, ybuf.at[1 - slot], sem.at[1 - slot])


def _combine(h1, ys, dest_tiles, wts, g2, b2, tm):
    n, d = h1.shape
    n_steps = n // tm
    row = lambda i, dst: (i, 0)
    const = lambda i, dst: (0, 0)
    return pl.pallas_call(
        functools.partial(_combine_kernel, tm=tm, n_steps=n_steps),
        grid_spec=pltpu.PrefetchScalarGridSpec(
            num_scalar_prefetch=1, grid=(n_steps,),
            in_specs=[pl.BlockSpec(memory_space=pl.ANY), pl.BlockSpec((tm, d), row),
                      pl.BlockSpec((tm, LANES), row), pl.BlockSpec((1, d), const), pl.BlockSpec((1, d), const)],
            out_specs=pl.BlockSpec((tm, d), row),
            scratch_shapes=[pltpu.VMEM((2, 2 * tm, d), F32), pltpu.SemaphoreType.DMA((2,))]),
        out_shape=jax.ShapeDtypeStruct((n, d), F32),
        name="combine_ln2",
        compiler_params=pltpu.CompilerParams(
            dimension_semantics=("arbitrary",), vmem_limit_bytes=VMEM_LIMIT),
    )(dest_tiles, ys, h1, wts, g2, b2)


def _dispatch_plan(e_ids, bm):
    n_tok = e_ids.shape[0]
    n_assign = n_tok * TOPK_IN_GROUP
    e = e_ids.reshape(-1)
    onehot = (e[:, None] == jnp.arange(N_EXPERTS, dtype=jnp.int32)[None, :]).astype(jnp.int32)
    cum = jnp.cumsum(onehot, axis=0)
    rank = jnp.take_along_axis(cum, e[:, None], axis=1)[:, 0] - 1
    counts = cum[-1]
    padded = (counts + bm - 1) // bm * bm
    pend = jnp.cumsum(padded)
    pstart = pend - padded
    dest = pstart[e] + rank
    n_blocks = -(-(n_assign + N_EXPERTS * (bm - 1)) // bm)
    n_rows = n_blocks * bm
    tok = jnp.repeat(jnp.arange(n_tok, dtype=jnp.int32), TOPK_IN_GROUP)
    row_tok = jnp.zeros((n_rows,), jnp.int32).at[dest].set(tok)
    block_pos = jnp.arange(n_blocks, dtype=jnp.int32) * bm
    block_e = jnp.sum((pend[None, :] <= block_pos[:, None]).astype(jnp.int32), axis=1)
    block_e = jnp.minimum(block_e, N_EXPERTS - 1)
    n_used = (pend[-1] // bm).astype(jnp.int32).reshape(1)
    return row_tok, dest.reshape(n_tok, TOPK_IN_GROUP), block_e, n_used


def kernel(x, w_in, b_in, hg_lb_logits, hg_norm_g, w_branch_a, w_branch_b, w_out, ln1_g, ln1_b,
           w_group, b_group, w_router, b_router, w_gate, w_up, w_down, ln2_g, ln2_b):
    bsz, seq, d = x.shape
    n = bsz * seq
    x2 = x.reshape(n, d)
    l = 0

    def packed(src):
        kw = src[..., MAIN_COLS:MAIN_COLS + IDX_DIM + IDX_HEADS]
        pad = jnp.zeros(src.shape[:-1] + (IKW_COLS - IDX_DIM - IDX_HEADS,), src.dtype)
        return jnp.concatenate([src[..., :MAIN_COLS], kw, pad, src[..., MAIN_COLS + IDX_DIM + IDX_HEADS:]], axis=-1)

    w_all = packed(w_in[l]).astype(BF16)
    tn = 1024 if w_all.shape[1] % 1024 == 0 else 512
    p32, p16 = _in_proj(x2, w_all, packed(b_in[l]).reshape(1, -1), tm=min(1024, n), tn=tn)

    branch_a = _hgrn(p32, hg_lb_logits, hg_norm_g[l].reshape(1, HG_WIDTH), bsz, seq, min(512, seq))

    v_t = p16[:, O16_V:O16_V + SA_WIDTH].reshape(bsz, seq, SA_WIDTH).transpose(0, 2, 1).reshape(bsz * SA_WIDTH, seq)
    tq = min(256, seq)
    branch_b = _dsa(p16, v_t, bsz, seq, tq=tq, tk=min(256, tq))

    wr = jnp.zeros((d, LANES), F32)
    wr = wr.at[:, :N_GROUPS].set(w_group[l]).at[:, N_GROUPS:N_GROUPS + N_EXPERTS].set(w_router[l])
    wr_hi = wr.astype(BF16)
    wr_lo = (wr - wr_hi.astype(F32)).astype(BF16)
    br = jnp.zeros((1, LANES), F32)
    br = br.at[0, :N_GROUPS].set(b_group[l]).at[0, N_GROUPS:N_GROUPS + N_EXPERTS].set(b_router[l])
    h1, ids, wts = _merge(branch_a, branch_b, p16, x2,
                          w_branch_a[l].astype(BF16), w_branch_b[l].astype(BF16), w_out[l].astype(BF16),
                          ln1_g[l].reshape(1, d), ln1_b[l].reshape(1, d), wr_hi, wr_lo, br, tm=min(512, n))
    tm = min(256, n)

    bm = 256
    row_tok, dest, block_e, n_used = _dispatch_plan(ids[:, :TOPK_IN_GROUP], bm)
    ys = _moe_ffn(h1, row_tok, block_e, n_used, w_gate[l], w_up[l], w_down[l], bm)
    dest_tiles = dest.reshape(n // tm, tm, TOPK_IN_GROUP).transpose(0, 2, 1).reshape(-1)
    out = _combine(h1, ys, dest_tiles, wts, ln2_g[l].reshape(1, d), ln2_b[l].reshape(1, d), tm)
    return out.reshape(bsz, seq, d)
```

```python
import functools
import math

import jax
import jax.numpy as jnp
from jax import lax
from jax.experimental import pallas as pl
from jax.experimental.pallas import tpu as pltpu

CHUNK = 64
HG_HEADS = 8
HG_DK = 128
HG_DV = 128
HG_WIDTH = HG_HEADS * HG_DK
SA_HEADS = 8
SA_HEAD_DIM = 128
SA_WIDTH = SA_HEADS * SA_HEAD_DIM
IDX_HEADS = 8
IDX_DIM = 64
IDX_WIDTH = IDX_HEADS * IDX_DIM
MAX_TOPK = 256
N_GROUPS = 4
EXPERTS_PER_GROUP = 8
N_EXPERTS = N_GROUPS * EXPERTS_PER_GROUP
TOPK_IN_GROUP = 2
DEPTH = 1
DN_ALPHA = (2.0 * DEPTH) ** 0.25
LN_EPS = 1e-5
RMS_EPS = 1e-6

LANES = 128
SUBLANES = 8
VMEM_LIMIT = 56 * 1024 * 1024
MASK_DIST = 1e30
M_INIT = -1e30

F32 = jnp.float32
BF16 = jnp.bfloat16

MAIN_COLS = 4 * HG_WIDTH + 3 * SA_WIDTH + IDX_WIDTH
IKW_COLS = 512
O16_LO = 4 * HG_WIDTH
O16_Q, O16_K, O16_V, O16_IQ, O16_IKW = 0, SA_WIDTH, 2 * SA_WIDTH, 3 * SA_WIDTH, 3 * SA_WIDTH + IDX_WIDTH
O16_GATES = 3 * SA_WIDTH + IDX_WIDTH + IKW_COLS


def _nt_dot(a, b):
    return lax.dot_general(a, b, (((1,), (1,)), ((), ())), preferred_element_type=F32)


def _tn_dot(a, b):
    return lax.dot_general(a, b, (((0,), (0,)), ((), ())), preferred_element_type=F32)


def _sigmoid(x):
    return 1.0 / (1.0 + jnp.exp(-x))


def _silu(x):
    return x * _sigmoid(x)


def _proj_kernel(x_ref, w_ref, b_ref, o32_ref, o16_ref, xb_ref, *, j16_lo):
    j = pl.program_id(1)

    @pl.when(j == 0)
    def _():
        xb_ref[...] = x_ref[...].astype(BF16)

    def block():
        return jnp.dot(xb_ref[...], w_ref[...], preferred_element_type=F32) + b_ref[...]

    @pl.when(j < j16_lo)
    def _():
        o32_ref[...] = block()

    @pl.when(j >= j16_lo)
    def _():
        o16_ref[...] = block().astype(BF16)


def _in_proj(x2, w_all, b_all, tm, tn):
    n, d = x2.shape
    cols = w_all.shape[1]
    lo = O16_LO // tn
    return pl.pallas_call(
        functools.partial(_proj_kernel, j16_lo=lo),
        grid=(n // tm, cols // tn),
        in_specs=[
            pl.BlockSpec((tm, d), lambda i, j: (i, 0)),
            pl.BlockSpec((d, tn), lambda i, j: (0, j)),
            pl.BlockSpec((1, tn), lambda i, j: (0, j)),
        ],
        out_specs=[
            pl.BlockSpec((tm, tn), lambda i, j: (i, jnp.minimum(j, lo - 1))),
            pl.BlockSpec((tm, tn), lambda i, j: (i, jnp.maximum(j - lo, 0))),
        ],
        out_shape=[jax.ShapeDtypeStruct((n, O16_LO), F32), jax.ShapeDtypeStruct((n, cols - O16_LO), BF16)],
        scratch_shapes=[pltpu.VMEM((tm, d), BF16)],
        name="in_proj",
        compiler_params=pltpu.CompilerParams(
            dimension_semantics=("parallel", "arbitrary"), vmem_limit_bytes=VMEM_LIMIT),
    )(x2, w_all, b_all)


def _hgrn_kernel(q_ref, f_ref, i_ref, g_ref, lbl_ref, ng_ref, o_ref, st_ref, *, n_chunks):
    @pl.when(pl.program_id(1) == 0)
    def _():
        st_ref[...] = jnp.zeros_like(st_ref)

    lbl = lbl_ref[...].astype(F32)
    lbe = jnp.exp(lbl - jnp.max(lbl, axis=0, keepdims=True))
    lb = lbe[0:1, :] / jnp.sum(lbe, axis=0, keepdims=True)

    r_i = lax.broadcasted_iota(jnp.int32, (CHUNK, CHUNK), 0)
    c_i = lax.broadcasted_iota(jnp.int32, (CHUNK, CHUNK), 1)
    causal = c_i <= r_i
    tri = jnp.where(causal, 1.0, 0.0).astype(BF16)

    def chunk_pair(c2, carry):
        heads = range(HG_HEADS)
        cols = [slice(h * HG_DK, (h + 1) * HG_DK) for h in heads]
        pre = []
        for j in range(2):
            r0 = pl.multiple_of((2 * c2 + j) * CHUNK, CHUNK)
            rows = pl.ds(r0, CHUNK)
            f = [lb[:, cols[h]] + (1.0 - lb[:, cols[h]]) * _sigmoid(f_ref[rows, cols[h]]) for h in heads]
            logf = [jnp.log(f[h]) for h in heads]
            hi = [logf[h].astype(BF16) for h in heads]
            lo = [(logf[h] - hi[h].astype(F32)).astype(BF16) for h in heads]
            cum = [jnp.dot(tri, hi[h], preferred_element_type=F32) + jnp.dot(tri, lo[h], preferred_element_type=F32)
                   for h in heads]
            cum_last = [cum[h][CHUNK - 1:CHUNK, :] for h in heads]
            vh = [i_ref[rows, cols[h]].astype(BF16) for h in heads]
            q_dec = [(_silu(q_ref[rows, cols[h]]) * jnp.exp(cum[h])).astype(BF16) for h in heads]
            k_dec = [((1.0 - f[h]) * jnp.exp(-cum[h])).astype(BF16) for h in heads]
            k_end = [((1.0 - f[h]) * jnp.exp(cum_last[h] - cum[h])).astype(BF16) for h in heads]
            att = [jnp.where(causal, _nt_dot(q_dec[h], k_dec[h]), 0.0).astype(BF16) for h in heads]
            kv = [_tn_dot(vh[h], k_end[h]) for h in heads]
            o_in = [jnp.dot(att[h], vh[h], preferred_element_type=F32) for h in heads]
            dec = [jnp.exp(cum_last[h]) for h in heads]
            pre.append((rows, q_dec, kv, o_in, dec))
        st = [st_ref[h] for h in heads]
        for rows, q_dec, kv, o_in, dec in pre:
            o = [o_in[h] + _nt_dot(q_dec[h], st[h].astype(BF16)) for h in heads]
            st = [st[h] * dec[h] + kv[h] for h in heads]
            for h in heads:
                on = o[h] * lax.rsqrt(jnp.mean(o[h] * o[h], axis=-1, keepdims=True) + RMS_EPS)
                o_ref[rows, cols[h]] = (on * ng_ref[:, cols[h]] * _silu(g_ref[rows, cols[h]])).astype(BF16)
        for h in heads:
            st_ref[h] = st[h]
        return carry

    lax.fori_loop(0, n_chunks // 2, chunk_pair, 0)


def _hgrn(p32, lb_logits, norm_g, bsz, seq, rows_per_step):
    n = bsz * seq
    w = HG_WIDTH
    steps = seq // rows_per_step
    assert rows_per_step % (2 * CHUNK) == 0
    nl = lb_logits.shape[0]

    def col_spec(cb):
        return pl.BlockSpec((rows_per_step, w), lambda b, s: (b * steps + s, cb))

    return pl.pallas_call(
        functools.partial(_hgrn_kernel, n_chunks=rows_per_step // CHUNK),
        grid=(bsz, steps),
        in_specs=[col_spec(0), col_spec(1), col_spec(2), col_spec(3),
                  pl.BlockSpec((nl, w), lambda b, s: (0, 0)),
                  pl.BlockSpec((1, w), lambda b, s: (0, 0))],
        out_specs=pl.BlockSpec((rows_per_step, w), lambda b, s: (b * steps + s, 0)),
        out_shape=jax.ShapeDtypeStruct((n, w), BF16),
        scratch_shapes=[pltpu.VMEM((HG_HEADS, HG_DV, HG_DK), F32)],
        name="hgrn",
        compiler_params=pltpu.CompilerParams(
            dimension_semantics=("parallel", "arbitrary"), vmem_limit_bytes=VMEM_LIMIT),
    )(p32, p32, p32, p32, lb_logits, norm_g)


def _dsa_kernel(qi_ref, qw_ref, q_ref, kiw_ref, k_ref, vt_ref, o_ref,
                s_scr, qs_scr, *acc_scrs, tq, tk, topk, t0, t_more, hg, kpb):
    i = pl.program_id(1)
    nkb = (i + 1) * (tq // tk)
    q0 = i * tq
    nsub = tk // SUBLANES

    qcol = q0 + lax.broadcasted_iota(jnp.int32, (1, tq), 1)
    kk = jnp.minimum((qcol // CHUNK + 1) * CHUNK, topk).astype(F32)

    w_t = (jnp.transpose(qw_ref[...].astype(F32))[IDX_DIM:IDX_DIM + IDX_HEADS, :]
           * (IDX_HEADS ** -0.5 * IDX_DIM ** -0.5))

    def tree(op, x):
        while x.shape[0] > 1:
            half = x.shape[0] // 2
            x = op(x[:half], x[half:])
        return x[0]

    def by_vreg(x):
        return x.reshape(nsub, SUBLANES, tq)

    tk1 = tk * kpb
    key1_t = lax.broadcasted_iota(jnp.int32, (tk1, tq), 0)
    qchunk1_t = (q0 + lax.broadcasted_iota(jnp.int32, (tk1, tq), 1)) // CHUNK

    def score_body(kb, carry):
        mn8, mx8 = carry
        k0 = pl.multiple_of(kb * tk1, tk1)
        kblk = kiw_ref[pl.ds(k0, tk1), :][:, :IDX_DIM]
        ds = [_nt_dot(kblk, qi_ref[:, h * IDX_DIM:(h + 1) * IDX_DIM]) for h in range(IDX_HEADS)]
        acc = w_t[0:1, :] * jnp.maximum(ds[0], 0.0)
        for h in range(1, IDX_HEADS):
            acc = acc + w_t[h:h + 1, :] * jnp.maximum(ds[h], 0.0)
        adm = (k0 + key1_t) // CHUNK <= qchunk1_t
        s_scr[pl.ds(k0, tk1), :] = jnp.where(adm, acc, -jnp.inf)
        by8 = lambda x: x.reshape(tk1 // SUBLANES, SUBLANES, tq)
        mn8 = jnp.minimum(mn8, tree(jnp.minimum, by8(jnp.where(adm, acc, jnp.inf))))
        mx8 = jnp.maximum(mx8, tree(jnp.maximum, by8(jnp.where(adm, acc, -jnp.inf))))
        return mn8, mx8

    mn8, mx8 = lax.fori_loop(0, (nkb + kpb - 1) // kpb, score_body,
                             (jnp.full((SUBLANES, tq), jnp.inf, F32), jnp.full((SUBLANES, tq), -jnp.inf, F32)))
    mn = jnp.min(mn8, axis=0, keepdims=True)
    mx = jnp.max(mx8, axis=0, keepdims=True)

    def fold(fn, init):
        def body(kb, c):
            k0 = pl.multiple_of(kb * tk, tk)
            return fn(c, s_scr[pl.ds(k0, tk), :], k0)
        return lax.fori_loop(0, nkb, body, init)

    def count(pred_fn):
        c = fold(lambda c, blk, k0: c + tree(jnp.add, by_vreg(jnp.where(pred_fn(blk, k0), 1.0, 0.0))),
                 jnp.zeros((SUBLANES, tq), F32))
        return jnp.sum(c, axis=0, keepdims=True)

    def any_lane(pred):
        return jnp.max(jnp.where(pred, 1.0, 0.0)) > 0.0

    def bisect(n_it, lo, hi, c_lo):
        def it(_, st):
            lo, hi, c_lo = st
            mid = 0.5 * lo + 0.5 * hi
            c = count(lambda blk, k0: blk >= mid)
            ok = c >= kk
            return jnp.where(ok, mid, lo), jnp.where(ok, hi, mid), jnp.where(ok, c, c_lo)
        return lax.fori_loop(0, n_it, it, (lo, hi, c_lo))

    def b_cond(st):
        lo, hi, c_lo, it = st
        return jnp.logical_and(any_lane(c_lo != kk), it < t0)

    def b_body(st):
        lo, hi, c_lo, it = st
        lo, hi, c_lo = bisect(2, lo, hi, c_lo)
        return lo, hi, c_lo, it + 2

    n_adm = ((qcol // CHUNK + 1) * CHUNK).astype(F32)
    lo, hi, c_lo, _ = lax.while_loop(b_cond, b_body, (mn, mx, n_adm, jnp.int32(0)))

    def penalty_plain():
        def body(kb, carry):
            k0 = pl.multiple_of(kb * tk, tk)
            s_scr[pl.ds(k0, tk), :] = jnp.where(s_scr[pl.ds(k0, tk), :] >= lo, 0.0, MASK_DIST)
            return carry
        lax.fori_loop(0, nkb, body, 0)

    def penalty_with_ties():
        def fold_min(sel_fn):
            c = fold(lambda c, blk, k0: jnp.minimum(c, tree(jnp.minimum, by_vreg(sel_fn(blk)))),
                     jnp.full((SUBLANES, tq), jnp.inf, F32))
            return jnp.min(c, axis=0, keepdims=True)

        def verify(lo):
            thr = fold_min(lambda blk: jnp.where(blk >= lo, blk, jnp.inf))
            return thr, count(lambda blk, k0: blk > thr)

        def w_cond(st):
            lo, hi, c_lo, thr, n_gt, it = st
            return jnp.logical_and(any_lane(n_gt >= kk), it < 64)

        def w_body(st):
            lo, hi, c_lo, thr, n_gt, it = st
            lo, hi, c_lo = bisect(t_more, lo, hi, c_lo)
            thr, n_gt = verify(lo)
            return lo, hi, c_lo, thr, n_gt, it + 1

        thr, n_gt = verify(lo)
        _, _, _, thr, n_gt, _ = lax.while_loop(w_cond, w_body, (lo, hi, c_lo, thr, n_gt, jnp.int32(0)))
        need = kk - n_gt
        r_i = lax.broadcasted_iota(jnp.int32, (tk, tk), 0)
        c_i = lax.broadcasted_iota(jnp.int32, (tk, tk), 1)
        earlier = jnp.where(c_i < r_i, 1.0, 0.0).astype(BF16)

        def body(kb, seen):
            k0 = pl.multiple_of(kb * tk, tk)
            blk = s_scr[pl.ds(k0, tk), :]
            tie = blk == thr
            tie01 = jnp.where(tie, 1.0, 0.0)
            rank = seen + jnp.dot(earlier, tie01.astype(BF16), preferred_element_type=F32)
            sel = jnp.logical_or(blk > thr, jnp.logical_and(tie, rank < need))
            s_scr[pl.ds(k0, tk), :] = jnp.where(sel, 0.0, MASK_DIST)
            return rank[tk - 1:tk, :] + tie01[tk - 1:tk, :]
        lax.fori_loop(0, nkb, body, jnp.zeros((1, tq), F32))

    lax.cond(any_lane(c_lo != kk), penalty_with_ties, penalty_plain)

    log2e = math.log2(math.e)
    qs_scr[...] = (q_ref[...].astype(F32) * (SA_HEAD_DIM ** -0.5 * log2e)).astype(BF16)
    for acc_scr in acc_scrs:
        acc_scr[...] = jnp.zeros(acc_scr.shape, F32)

    tk4 = tk * kpb
    nkb4 = (nkb + kpb - 1) // kpb
    key4_t = lax.broadcasted_iota(jnp.int32, (tk4, tq), 0)
    qry4_t = q0 + lax.broadcasted_iota(jnp.int32, (tk4, tq), 1)

    @pl.when(nkb4 * kpb > nkb)
    def _():
        s_scr[pl.ds(pl.multiple_of(nkb * tk, tk), tk), :] = jnp.full((tk, tq), MASK_DIST, F32)

    def att_body(kb, carry):
        m_all, l_all = carry
        k0 = pl.multiple_of(kb * tk4, tk4)
        dist = jnp.abs(qry4_t - (k0 + key4_t)).astype(F32) + s_scr[pl.ds(k0, tk4), :]
        cols = [slice(h * SA_HEAD_DIM, (h + 1) * SA_HEAD_DIM) for h in range(SA_HEADS)]
        m_rows, l_rows = [], []
        for g0 in range(0, SA_HEADS, hg):
            heads = range(g0, g0 + hg)
            qk = {h: _nt_dot(k_ref[pl.ds(k0, tk4), cols[h]], qs_scr[:, cols[h]]) for h in heads}
            alphas, ps = {}, {}
            for h in heads:
                slope2 = 2.0 ** (-8.0 * (h + 1) / SA_HEADS) * log2e
                logits = qk[h] - slope2 * dist
                m_old = m_all[h:h + 1, :]
                m_new = jnp.maximum(m_old, jnp.max(logits, axis=0, keepdims=True))
                alphas[h] = jnp.exp2(m_old - m_new)
                p = jnp.exp2(logits - m_new)
                l_rows.append(alphas[h] * l_all[h:h + 1, :] + jnp.sum(p, axis=0, keepdims=True))
                m_rows.append(m_new)
                ps[h] = p.astype(BF16)
            pv = {h: jnp.dot(vt_ref[cols[h], pl.ds(k0, tk4)], ps[h], preferred_element_type=F32) for h in heads}
            for h in heads:
                acc_scrs[h][...] = alphas[h] * acc_scrs[h][...] + pv[h]
        return jnp.concatenate(m_rows, axis=0), jnp.concatenate(l_rows, axis=0)

    _, l_all = lax.fori_loop(0, nkb4, att_body,
                             (jnp.full((SA_HEADS, tq), M_INIT, F32), jnp.zeros((SA_HEADS, tq), F32)))

    for h in range(SA_HEADS):
        o_ref[:, h * SA_HEAD_DIM:(h + 1) * SA_HEAD_DIM] = jnp.transpose(acc_scrs[h][...] / l_all[h:h + 1, :]).astype(BF16)


def _dsa(p16, v_t, bsz, seq, tq, tk):
    n = bsz * seq
    steps = seq // tq
    topk = min(MAX_TOPK, seq // 4)
    kpb = 2
    assert seq % (tk * kpb) == 0 and tq % tk == 0
    once = dict(pipeline_mode=pl.Buffered(1))
    qrow = lambda cb: (lambda b, i: (b * steps + i, cb))
    return pl.pallas_call(
        functools.partial(_dsa_kernel, tq=tq, tk=tk, topk=topk, t0=22, t_more=4, hg=SA_HEADS, kpb=kpb),
        grid=(bsz, steps),
        in_specs=[
            pl.BlockSpec((tq, IDX_WIDTH), qrow(O16_IQ // IDX_WIDTH)),
            pl.BlockSpec((tq, LANES), qrow(O16_IKW // LANES)),
            pl.BlockSpec((tq, SA_WIDTH), qrow(O16_Q // SA_WIDTH)),
            pl.BlockSpec((seq, LANES), lambda b, i: (b, O16_IKW // LANES), **once),
            pl.BlockSpec((seq, SA_WIDTH), lambda b, i: (b, O16_K // SA_WIDTH), **once),
            pl.BlockSpec((SA_WIDTH, seq), lambda b, i: (b, 0), **once),
        ],
        out_specs=pl.BlockSpec((tq, SA_WIDTH), lambda b, i: (b * steps + i, 0)),
        out_shape=jax.ShapeDtypeStruct((n, SA_WIDTH), BF16),
        scratch_shapes=[
            pltpu.VMEM((seq, tq), F32),
            pltpu.VMEM((tq, SA_WIDTH), BF16),
        ] + [pltpu.VMEM((SA_HEAD_DIM, tq), F32) for _ in range(SA_HEADS)],
        name="dsa",
        compiler_params=pltpu.CompilerParams(
            dimension_semantics=("parallel", "arbitrary"), vmem_limit_bytes=VMEM_LIMIT),
    )(p16, p16, p16, p16, p16, v_t)


def _layer_norm(x, g, b):
    mu = jnp.mean(x, axis=-1, keepdims=True)
    xc = x - mu
    var = jnp.mean(xc * xc, axis=-1, keepdims=True)
    return xc * lax.rsqrt(var + LN_EPS) * g + b


def _merge_kernel(a_ref, b_ref, ga_ref, gb_ref, x_ref, wa_ref, wb_ref, wo_ref, g1_ref, b1_ref,
                  wrh_ref, wrl_ref, br_ref, h_ref, ids_ref, wts_ref):
    merged = (_sigmoid(ga_ref[...].astype(F32)) * jnp.dot(a_ref[...], wa_ref[...], preferred_element_type=F32)
              + _sigmoid(gb_ref[...].astype(F32)) * jnp.dot(b_ref[...], wb_ref[...], preferred_element_type=F32))
    mix = jnp.dot(merged.astype(BF16), wo_ref[...], preferred_element_type=F32)
    h = _layer_norm(DN_ALPHA * x_ref[...] + mix, g1_ref[...], b1_ref[...])
    h_ref[...] = h

    h_hi = h.astype(BF16)
    h_lo = (h - h_hi.astype(F32)).astype(BF16)
    logits = (jnp.dot(h_hi, wrh_ref[...], preferred_element_type=F32)
              + jnp.dot(h_lo, wrh_ref[...], preferred_element_type=F32)
              + jnp.dot(h_hi, wrl_ref[...], preferred_element_type=F32)) + br_ref[...]
    lane = lax.broadcasted_iota(jnp.int32, logits.shape, 1)
    big = jnp.int32(4 * LANES)

    def first_argmax(vals):
        mval = jnp.max(vals, axis=-1, keepdims=True)
        idx = jnp.min(jnp.where(vals == mval, lane, big), axis=-1, keepdims=True)
        return mval, idx

    gl = jnp.where(lane < N_GROUPS, logits, -jnp.inf)
    g_max, g_idx = first_argmax(gl)
    g_top_p = 1.0 / jnp.sum(jnp.exp(gl - g_max), axis=-1, keepdims=True)
    e_lo = N_GROUPS + g_idx * EXPERTS_PER_GROUP
    el = jnp.where(jnp.logical_and(lane >= e_lo, lane < e_lo + EXPERTS_PER_GROUP), logits, -jnp.inf)
    e1, i1 = first_argmax(el)
    e2, i2 = first_argmax(jnp.where(lane == i1, -jnp.inf, el))
    t = jnp.exp(e2 - e1)
    w1 = (1.0 / (1.0 + t)) * g_top_p
    w2 = (t / (1.0 + t)) * g_top_p
    ids_ref[...] = jnp.where(lane == 0, i1 - N_GROUPS, jnp.where(lane == 1, i2 - N_GROUPS, 0))
    wts_ref[...] = jnp.where(lane == 0, w1, jnp.where(lane == 1, w2, 0.0))


def _merge(branch_a, branch_b, p16, x2, wa, wb, wo, g1, b1, wr_hi, wr_lo, br, tm):
    n, d = x2.shape
    once = dict(pipeline_mode=pl.Buffered(1))
    ga_blk = O16_GATES // d
    row = lambda i: (i, 0)
    const = lambda i: (0, 0)
    return pl.pallas_call(
        _merge_kernel,
        grid=(n // tm,),
        in_specs=[
            pl.BlockSpec((tm, HG_WIDTH), row),
            pl.BlockSpec((tm, SA_WIDTH), row),
            pl.BlockSpec((tm, d), lambda i: (i, ga_blk)),
            pl.BlockSpec((tm, d), lambda i: (i, ga_blk + 1)),
            pl.BlockSpec((tm, d), row),
            pl.BlockSpec((HG_WIDTH, d), const, **once),
            pl.BlockSpec((SA_WIDTH, d), const, **once),
            pl.BlockSpec((d, d), const, **once),
            pl.BlockSpec((1, d), const),
            pl.BlockSpec((1, d), const),
            pl.BlockSpec((d, LANES), const, **once),
            pl.BlockSpec((d, LANES), const, **once),
            pl.BlockSpec((1, LANES), const),
        ],
        out_specs=[pl.BlockSpec((tm, d), row), pl.BlockSpec((tm, LANES), row), pl.BlockSpec((tm, LANES), row)],
        out_shape=[jax.ShapeDtypeStruct((n, d), F32),
                   jax.ShapeDtypeStruct((n, LANES), jnp.int32),
                   jax.ShapeDtypeStruct((n, LANES), F32)],
        name="merge_ln1_router",
        compiler_params=pltpu.CompilerParams(
            dimension_semantics=("parallel",), vmem_limit_bytes=VMEM_LIMIT),
    )(branch_a, branch_b, p16, p16, x2, wa, wb, wo, g1, b1, wr_hi, wr_lo, br)


def _row_gather(idx_ref, base, n_rows, src_hbm, dst, sem):
    for r in range(n_rows):
        pltpu.make_async_copy(src_hbm.at[pl.ds(idx_ref[base + r], 1)], dst.at[pl.ds(r, 1)], sem).start()


def _row_gather_wait(n_rows, src_hbm, dst, sem):
    pltpu.make_async_copy(src_hbm.at[pl.ds(0, n_rows)], dst, sem).wait()


def _moe_up_kernel(be_ref, nu_ref, rt_ref, h_hbm, wg_ref, wu_ref, o_ref, xbuf, sem, *, bm, n_blocks):
    i = pl.program_id(0)
    nu = nu_ref[0]
    slot = i % 2

    @pl.when(i == 0)
    def _():
        _row_gather(rt_ref, 0, bm, h_hbm, xbuf.at[0], sem.at[0])

    @pl.when(i < nu)
    def _():
        nxt = jnp.minimum(i + 1, n_blocks - 1)
        _row_gather(rt_ref, nxt * bm, bm, h_hbm, xbuf.at[1 - slot], sem.at[1 - slot])
        _row_gather_wait(bm, h_hbm, xbuf.at[slot], sem.at[slot])
        xb = xbuf[slot].astype(BF16)
        g = jnp.dot(xb, wg_ref[0], preferred_element_type=F32)
        u = jnp.dot(xb, wu_ref[0], preferred_element_type=F32)
        o_ref[...] = (_silu(g) * u).astype(BF16)

        @pl.when(i == n_blocks - 1)
        def _():
            _row_gather_wait(bm, h_hbm, xbuf.at[1 - slot], sem.at[1 - slot])

    @pl.when(i >= nu)
    def _():
        @pl.when(i == nu)
        def _():
            _row_gather_wait(bm, h_hbm, xbuf.at[slot], sem.at[slot])
        o_ref[...] = jnp.zeros_like(o_ref)


def _moe_down_kernel(be_ref, nu_ref, h_ref, wd_ref, o_ref):
    @pl.when(pl.program_id(0) < nu_ref[0])
    def _():
        o_ref[...] = jnp.dot(h_ref[...], wd_ref[0], preferred_element_type=F32)

    @pl.when(pl.program_id(0) >= nu_ref[0])
    def _():
        o_ref[...] = jnp.zeros_like(o_ref)


def _moe_ffn(h1, row_tok, block_e, n_used, wg, wu, wd, bm):
    n_rows = row_tok.shape[0]
    d = h1.shape[1]
    f = wg.shape[2]
    n_blocks = n_rows // bm
    params = pltpu.CompilerParams(dimension_semantics=("arbitrary",), vmem_limit_bytes=VMEM_LIMIT)
    hb = pl.pallas_call(
        functools.partial(_moe_up_kernel, bm=bm, n_blocks=n_blocks),
        grid_spec=pltpu.PrefetchScalarGridSpec(
            num_scalar_prefetch=3, grid=(n_blocks,),
            in_specs=[pl.BlockSpec(memory_space=pl.ANY),
                      pl.BlockSpec((1, d, f), lambda i, be, nu, rt: (be[i], 0, 0)),
                      pl.BlockSpec((1, d, f), lambda i, be, nu, rt: (be[i], 0, 0))],
            out_specs=pl.BlockSpec((bm, f), lambda i, be, nu, rt: (i, 0)),
            scratch_shapes=[pltpu.VMEM((2, bm, d), F32), pltpu.SemaphoreType.DMA((2,))]),
        out_shape=jax.ShapeDtypeStruct((n_rows, f), BF16),
        name="moe_up", compiler_params=params,
    )(block_e, n_used, row_tok, h1, wg, wu)
    return pl.pallas_call(
        _moe_down_kernel,
        grid_spec=pltpu.PrefetchScalarGridSpec(
            num_scalar_prefetch=2, grid=(n_blocks,),
            in_specs=[pl.BlockSpec((bm, f), lambda i, be, nu: (i, 0)),
                      pl.BlockSpec((1, f, d), lambda i, be, nu: (be[i], 0, 0))],
            out_specs=pl.BlockSpec((bm, d), lambda i, be, nu: (i, 0))),
        out_shape=jax.ShapeDtypeStruct((n_rows, d), F32),
        name="moe_down", compiler_params=params,
    )(block_e, n_used, hb, wd)


def _combine_kernel(dest_ref, ys_hbm, h_ref, wts_ref, g2_ref, b2_ref, o_ref, ybuf, sem, *, tm, n_steps):
    i = pl.program_id(0)
    slot = i % 2

    @pl.when(i == 0)
    def _():
        _row_gather(dest_ref, 0, 2 * tm, ys_hbm, ybuf.at[0], sem.at[0])

    nxt = jnp.minimum(i + 1, n_steps - 1)
    _row_gather(dest_ref, nxt * 2 * tm, 2 * tm, ys_hbm, ybuf.at[1 - slot], sem.at[1 - slot])
    _row_gather_wait(2 * tm, ys_hbm, ybuf.at[slot], sem.at[slot])
    w0 = wts_ref[:, 0:1]
    w1 = wts_ref[:, 1:2]
    ffn = w0 * ybuf[slot, 0:tm, :] + w1 * ybuf[slot, tm:2 * tm, :]
    o_ref[...] = _layer_norm(DN_ALPHA * h_ref[...] + ffn, g2_ref[...], b2_ref[...])

    @pl.when(i == n_steps - 1)
    def _():
        _row_gather_wait(2 * tm, ys_hbm, ybuf.at[1 - slot], sem.at[1 - slot])


def _combine(h1, ys, dest_tiles, wts, g2, b2, tm):
    n, d = h1.shape
    n_steps = n // tm
    row = lambda i, dst: (i, 0)
    const = lambda i, dst: (0, 0)
    return pl.pallas_call(
        functools.partial(_combine_kernel, tm=tm, n_steps=n_steps),
        grid_spec=pltpu.PrefetchScalarGridSpec(
            num_scalar_prefetch=1, grid=(n_steps,),
            in_specs=[pl.BlockSpec(memory_space=pl.ANY), pl.BlockSpec((tm, d), row),
                      pl.BlockSpec((tm, LANES), row), pl.BlockSpec((1, d), const), pl.BlockSpec((1, d), const)],
            out_specs=pl.BlockSpec((tm, d), row),
            scratch_shapes=[pltpu.VMEM((2, 2 * tm, d), F32), pltpu.SemaphoreType.DMA((2,))]),
        out_shape=jax.ShapeDtypeStruct((n, d), F32),
        name="combine_ln2",
        compiler_params=pltpu.CompilerParams(
            dimension_semantics=("arbitrary",), vmem_limit_bytes=VMEM_LIMIT),
    )(dest_tiles, ys, h1, wts, g2, b2)


def _dispatch_plan(e_ids, bm):
    n_tok = e_ids.shape[0]
    n_assign = n_tok * TOPK_IN_GROUP
    e = e_ids.reshape(-1)
    onehot = (e[:, None] == jnp.arange(N_EXPERTS, dtype=jnp.int32)[None, :]).astype(jnp.int32)
    cum = jnp.cumsum(onehot, axis=0)
    rank = jnp.take_along_axis(cum, e[:, None], axis=1)[:, 0] - 1
    counts = cum[-1]
    padded = (counts + bm - 1) // bm * bm
    pend = jnp.cumsum(padded)
    pstart = pend - padded
    dest = pstart[e] + rank
    n_blocks = -(-(n_assign + N_EXPERTS * (bm - 1)) // bm)
    n_rows = n_blocks * bm
    tok = jnp.repeat(jnp.arange(n_tok, dtype=jnp.int32), TOPK_IN_GROUP)
    row_tok = jnp.zeros((n_rows,), jnp.int32).at[dest].set(tok)
    block_pos = jnp.arange(n_blocks, dtype=jnp.int32) * bm
    block_e = jnp.sum((pend[None, :] <= block_pos[:, None]).astype(jnp.int32), axis=1)
    block_e = jnp.minimum(block_e, N_EXPERTS - 1)
    n_used = (pend[-1] // bm).astype(jnp.int32).reshape(1)
    return row_tok, dest.reshape(n_tok, TOPK_IN_GROUP), block_e, n_used


def kernel(x, w_in, b_in, hg_lb_logits, hg_norm_g, w_branch_a, w_branch_b, w_out, ln1_g, ln1_b,
           w_group, b_group, w_router, b_router, w_gate, w_up, w_down, ln2_g, ln2_b):
    bsz, seq, d = x.shape
    n = bsz * seq
    x2 = x.reshape(n, d)
    l = 0

    def packed(src):
        kw = src[..., MAIN_COLS:MAIN_COLS + IDX_DIM + IDX_HEADS]
        pad = jnp.zeros(src.shape[:-1] + (IKW_COLS - IDX_DIM - IDX_HEADS,), src.dtype)
        return jnp.concatenate([src[..., :MAIN_COLS], kw, pad, src[..., MAIN_COLS + IDX_DIM + IDX_HEADS:]], axis=-1)

    w_all = packed(w_in[l]).astype(BF16)
    tn = 1024 if w_all.shape[1] % 1024 == 0 else 512
    p32, p16 = _in_proj(x2, w_all, packed(b_in[l]).reshape(1, -1), tm=min(1024, n), tn=tn)

    branch_a = _hgrn(p32, hg_lb_logits, hg_norm_g[l].reshape(1, HG_WIDTH), bsz, seq, min(512, seq))

    v_t = p16[:, O16_V:O16_V + SA_WIDTH].reshape(bsz, seq, SA_WIDTH).transpose(0, 2, 1).reshape(bsz * SA_WIDTH, seq)
    tq = min(256, seq)
    branch_b = _dsa(p16, v_t, bsz, seq, tq=tq, tk=min(256, tq))

    wr = jnp.zeros((d, LANES), F32)
    wr = wr.at[:, :N_GROUPS].set(w_group[l]).at[:, N_GROUPS:N_GROUPS + N_EXPERTS].set(w_router[l])
    wr_hi = wr.astype(BF16)
    wr_lo = (wr - wr_hi.astype(F32)).astype(BF16)
    br = jnp.zeros((1, LANES), F32)
    br = br.at[0, :N_GROUPS].set(b_group[l]).at[0, N_GROUPS:N_GROUPS + N_EXPERTS].set(b_router[l])
    h1, ids, wts = _merge(branch_a, branch_b, p16, x2,
                          w_branch_a[l].astype(BF16), w_branch_b[l].astype(BF16), w_out[l].astype(BF16),
                          ln1_g[l].reshape(1, d), ln1_b[l].reshape(1, d), wr_hi, wr_lo, br, tm=min(512, n))
    tm = min(256, n)

    bm = 256
    row_tok, dest, block_e, n_used = _dispatch_plan(ids[:, :TOPK_IN_GROUP], bm)
    ys = _moe_ffn(h1, row_tok, block_e, n_used, w_gate[l], w_up[l], w_down[l], bm)
    dest_tiles = dest.reshape(n // tm, tm, TOPK_IN_GROUP).transpose(0, 2, 1).reshape(-1)
    out = _combine(h1, ys, dest_tiles, wts, ln2_g[l].reshape(1, d), ln2_b[l].reshape(1, d), tm)
    return out.reshape(bsz, seq, d)
```
